```python
import math
import jax, jax.numpy as jnp
from jax import lax
import numpy as np

D_MODEL = 1024
BATCH = 4
SEQ = 8192
DEPTH = 4

N_MIXERS = 3
MEM_LEN = 256
EPS = 1e-6
DA_HEADS = 8
DA_QK_DIM = 64
DA_V_DIM = 2 * DA_QK_DIM
Q_BLOCK = 128
REL_BUCKETS = 32
REL_MAX_DIST = 128
HG_EXPAND = 128
HG_HEADS = D_MODEL // HG_EXPAND
HG_V_DIM = D_MODEL // HG_HEADS
HG_CHUNK = 64
SG_CHUNK = 128
SG_GROUPS = 8
SG_WIDTH = D_MODEL
SG_GROUP_DIM = SG_WIDTH // SG_GROUPS
CA_HEADS = 4
CA_HEAD_DIM = D_MODEL // CA_HEADS
D_FF = int(math.ceil(8 * D_MODEL / 3 / 256)) * 256
N_A = (DEPTH + 2) // 3
N_B = (DEPTH + 1) // 3
N_C = DEPTH // 3

kernel_name = "hybrid_diffattn_hgrn2_gmlp_trunk"


def rmsnorm(x, g):
    xf = x.astype(jnp.float32)
    y = xf * lax.rsqrt(jnp.mean(xf * xf, axis=-1, keepdims=True) + EPS)
    return (y * g.astype(jnp.float32)).astype(x.dtype)


def rel_bucket(dist):
    n = jnp.maximum(dist, 0)
    exact = REL_BUCKETS // 2
    nf = jnp.maximum(n, exact).astype(jnp.float32)
    large = exact + (jnp.log(nf / exact) / math.log(REL_MAX_DIST / exact)
                     * (REL_BUCKETS - exact)).astype(jnp.int32)
    large = jnp.minimum(large, REL_BUCKETS - 1)
    return jnp.where(n < exact, n, large)


def diff_attention(h, w_in, w_out, lq1, lk1, lq2, lk2, subln_g, rel_bias, layer_idx):
    B, S, _ = h.shape
    q, k, v = jnp.split(h @ w_in, 3, axis=-1)
    q = q.reshape(B, S, DA_HEADS, 2, DA_QK_DIM)
    k = k.reshape(B, S, DA_HEADS, 2, DA_QK_DIM)
    v = v.reshape(B, S, DA_HEADS, DA_V_DIM)
    lam_init = 0.8 - 0.6 * math.exp(-0.3 * layer_idx)
    lam = (jnp.exp(jnp.sum(lq1.astype(jnp.float32) * lk1.astype(jnp.float32)))
           - jnp.exp(jnp.sum(lq2.astype(jnp.float32) * lk2.astype(jnp.float32))) + lam_init)
    n_blk = S // Q_BLOCK
    qb = q.reshape(B, n_blk, Q_BLOCK, DA_HEADS, 2, DA_QK_DIM).transpose(1, 0, 2, 3, 4, 5)
    starts = jnp.arange(n_blk, dtype=jnp.int32) * Q_BLOCK
    k_pos = jnp.arange(S, dtype=jnp.int32)
    scale = DA_QK_DIM ** -0.5

    def block(args):
        q_blk, start = args
        dist = (start + jnp.arange(Q_BLOCK, dtype=jnp.int32))[:, None] - k_pos[None, :]
        bias = rel_bias.astype(jnp.float32)[:, rel_bucket(dist)]
        logits = jnp.einsum('bqhcd,bkhcd->bhcqk', q_blk, k).astype(jnp.float32) * scale
        logits = logits + bias[None, :, None]
        logits = jnp.where(dist >= 0, logits, -jnp.inf)
        p = jax.nn.softmax(logits, axis=-1)
        attn = p[:, :, 0] - lam * p[:, :, 1]
        return jnp.einsum('bhqk,bkhd->bqhd', attn.astype(v.dtype), v)

    o = lax.map(block, (qb, starts))
    o = o.transpose(1, 0, 2, 3, 4).reshape(B, S, DA_HEADS, DA_V_DIM)
    o = rmsnorm(o, subln_g) * (1.0 - lam_init)
    return o.reshape(B, S, D_MODEL).astype(h.dtype) @ w_out


def hgrn2(h, w_in, w_out, lower_bound, onorm_g):
    B, S, _ = h.shape
    q, f, i, g = jnp.split(h @ w_in, 4, axis=-1)
    lb = lower_bound.astype(jnp.float32)
    ff = f.astype(jnp.float32)
    log_f = jnp.logaddexp(jnp.log(lb), jnp.log1p(-lb) + jax.nn.log_sigmoid(ff))
    key = 1.0 - jnp.exp(log_f)
    qf = jax.nn.silu(q.astype(jnp.float32))
    vf = i.astype(jnp.float32)
    n_ch = S // HG_CHUNK

    def to_chunks(t, d):
        return t.reshape(B, n_ch, HG_CHUNK, HG_HEADS, d).transpose(1, 0, 3, 2, 4)

    qc, kc, gc = (to_chunks(t, HG_EXPAND) for t in (qf, key, log_f))
    vc = to_chunks(vf, HG_V_DIM)
    causal = jnp.tril(jnp.ones((HG_CHUNK, HG_CHUNK), dtype=bool))

    def step(state, xs):
        qt, kt, vt, gt = xs
        G = jnp.cumsum(gt, axis=2)
        o_inter = jnp.einsum('bhtk,bhkv->bhtv', qt * jnp.exp(G), state)
        diff = G[:, :, :, None, :] - G[:, :, None, :, :]
        decay = jnp.exp(jnp.where(causal[:, :, None], diff, -jnp.inf))
        A = jnp.einsum('bhtk,bhsk,bhtsk->bhts', qt, kt, decay)
        o_intra = jnp.einsum('bhts,bhsv->bhtv', A, vt)
        G_last = G[:, :, -1]
        k_dec = kt * jnp.exp(G_last[:, :, None] - G)
        new_state = jnp.exp(G_last)[..., None] * state + jnp.einsum('bhsk,bhsv->bhkv', k_dec, vt)
        return new_state, o_inter + o_intra

    s0 = jnp.zeros((B, HG_HEADS, HG_EXPAND, HG_V_DIM), jnp.float32)
    _, o = lax.scan(step, s0, (qc, kc, vc, gc))
    o = o.transpose(1, 0, 3, 2, 4).reshape(B, S, HG_HEADS, HG_V_DIM).astype(h.dtype)
    o = rmsnorm(o, onorm_g).reshape(B, S, D_MODEL) * jax.nn.silu(g)
    return o @ w_out


def chunked_sgu(h, w_in, w_out, vnorm_g, w_s, b_s):
    B, S, _ = h.shape
    u, v = jnp.split(jax.nn.gelu(h @ w_in, approximate=False), 2, axis=-1)
    v = rmsnorm(v, vnorm_g)
    n = S // SG_CHUNK
    v = v.reshape(B, n, SG_CHUNK, SG_GROUPS, SG_GROUP_DIM)
    w = w_s * jnp.tril(jnp.ones((SG_CHUNK, SG_CHUNK), w_s.dtype))
    mixed = jnp.einsum('gts,bnsgc->bntgc', w, v) + b_s.T[None, None, :, :, None]
    return (u * mixed.reshape(B, S, SG_WIDTH)) @ w_out


def mem_cross_attention(h, mem_n, w_q, w_kv, w_o):
    B, S, _ = h.shape
    q = (h @ w_q).reshape(B, S, CA_HEADS, CA_HEAD_DIM)
    k, v = jnp.split(mem_n @ w_kv, 2, axis=-1)
    k = k.reshape(B, -1, CA_HEADS, CA_HEAD_DIM)
    v = v.reshape(B, -1, CA_HEADS, CA_HEAD_DIM)
    logits = jnp.einsum('bshd,bmhd->bhsm', q, k).astype(jnp.float32) * (CA_HEAD_DIM ** -0.5)
    p = jax.nn.softmax(logits, axis=-1)
    o = jnp.einsum('bhsm,bmhd->bshd', p.astype(v.dtype), v).reshape(B, S, D_MODEL)
    return o @ w_o


def swiglu(h, w_gu, w_down):
    gate, up = jnp.split(h @ w_gu, 2, axis=-1)
    return (jax.nn.silu(gate) * up) @ w_down


def setup_inputs(seed: int = 0) -> dict:
    key = jax.random.key(seed)
    k = jax.random.split(key, 32)
    f32 = jnp.float32

    def nrm(kk, shape, scale):
        return jax.random.normal(kk, shape, f32) * scale

    def gain(kk, shape):
        return 1.0 + nrm(kk, shape, 0.02)

    D = D_MODEL
    return {
        "x": nrm(k[0], (BATCH, SEQ, D), 1.0),
        "mem": nrm(k[1], (BATCH, MEM_LEN, D), 1.0),
        "rel_bias": nrm(k[2], (DA_HEADS, REL_BUCKETS), 0.5),
        "norm_mix": gain(k[3], (DEPTH, D)),
        "norm_cross": gain(k[4], (DEPTH, D)),
        "norm_ffn": gain(k[5], (DEPTH, D)),
        "norm_mem": gain(k[6], (D,)),
        "norm_final": gain(k[7], (D,)),
        "da_w_in": nrm(k[8], (N_A, D, 3 * D), D ** -0.5),
        "da_w_out": nrm(k[9], (N_A, D, D), D ** -0.5),
        "da_lq1": nrm(k[10], (N_A, DA_QK_DIM), 0.1),
        "da_lk1": nrm(k[11], (N_A, DA_QK_DIM), 0.1),
        "da_lq2": nrm(k[12], (N_A, DA_QK_DIM), 0.1),
        "da_lk2": nrm(k[13], (N_A, DA_QK_DIM), 0.1),
        "da_subln": gain(k[14], (N_A, DA_V_DIM)),
        "hg_w_in": nrm(k[15], (N_B, D, 4 * D), D ** -0.5),
        "hg_w_out": nrm(k[16], (N_B, D, D), D ** -0.5),
        "hg_lower_bounds": nrm(k[17], (DEPTH, HG_HEADS * HG_EXPAND), 0.5),
        "hg_onorm": gain(k[18], (N_B, HG_V_DIM)),
        "sg_w_in": nrm(k[19], (N_C, D, 2 * SG_WIDTH), D ** -0.5),
        "sg_w_out": nrm(k[20], (N_C, SG_WIDTH, D), SG_WIDTH ** -0.5),
        "sg_vnorm": gain(k[21], (N_C, SG_WIDTH)),
        "sg_w_s": nrm(k[22], (N_C, SG_GROUPS, SG_CHUNK, SG_CHUNK), SG_CHUNK ** -0.5),
        "sg_b_s": 1.0 + nrm(k[23], (N_C, SG_GROUPS, SG_CHUNK), 0.1),
        "ca_w_q": nrm(k[24], (DEPTH, D, D), D ** -0.5),
        "ca_w_kv": nrm(k[25], (DEPTH, D, 2 * D), D ** -0.5),
        "ca_w_o": nrm(k[26], (DEPTH, D, D), D ** -0.5),
        "ffn_w_gu": nrm(k[27], (DEPTH, D, 2 * D_FF), D ** -0.5),
        "ffn_w_down": nrm(k[28], (DEPTH, D_FF, D), D_FF ** -0.5),
    }


def reference(x, mem, rel_bias, norm_mix, norm_cross, norm_ffn, norm_mem, norm_final,
              da_w_in, da_w_out, da_lq1, da_lk1, da_lq2, da_lk2, da_subln,
              hg_w_in, hg_w_out, hg_lower_bounds, hg_onorm,
              sg_w_in, sg_w_out, sg_vnorm, sg_w_s, sg_b_s,
              ca_w_q, ca_w_kv, ca_w_o, ffn_w_gu, ffn_w_down):
    mem_n = rmsnorm(mem, norm_mem)
    lb = jax.nn.softmax(hg_lower_bounds.astype(jnp.float32), axis=0)
    lb = jnp.cumsum(lb, axis=0) - lb[0]
    for i in range(DEPTH):
        kind = i % N_MIXERS
        j = i // N_MIXERS
        hn = rmsnorm(x, norm_mix[i])
        if kind == 0:
            mix = diff_attention(hn, da_w_in[j], da_w_out[j], da_lq1[j], da_lk1[j],
                                 da_lq2[j], da_lk2[j], da_subln[j], rel_bias, i)
        elif kind == 1:
            mix = hgrn2(hn, hg_w_in[j], hg_w_out[j], lb[i], hg_onorm[j])
        else:
            mix = chunked_sgu(hn, sg_w_in[j], sg_w_out[j], sg_vnorm[j], sg_w_s[j], sg_b_s[j])
        x = x + mix
        x = x + mem_cross_attention(rmsnorm(x, norm_cross[i]), mem_n, ca_w_q[i], ca_w_kv[i], ca_w_o[i])
        x = x + swiglu(rmsnorm(x, norm_ffn[i]), ffn_w_gu[i], ffn_w_down[i])
    return rmsnorm(x, norm_final)
```

```python
import functools
import math

import numpy as np
import jax
import jax.numpy as jnp
from jax import lax
from jax.experimental import pallas as pl
from jax.experimental.pallas import tpu as pltpu

F32 = jnp.float32
BF16 = jnp.bfloat16

EPS = 1e-6
LOG2E = math.log2(math.e)
NEG_BIG = -1e30

DA_HEADS = 8
DA_QK_DIM = 64
DA_V_DIM = 128
REL_BUCKETS = 32
REL_MAX_DIST = 128
HG_HEADS = 8
HG_DIM = 128
HG_CHUNK = 64
HG_SUB = 16
SG_CHUNK = 128
SG_GROUPS = 8
CA_HEADS = 4
N_MIXERS = 3

VMEM_LIMIT = 56 * 1024 * 1024
TOK_TILE = 512
ATT_TILE = 512
HG_TILE = 512
FFN_CHUNK = 256


def _params(sem):
    return pltpu.CompilerParams(dimension_semantics=sem, vmem_limit_bytes=VMEM_LIMIT)


def _resident(shape):
    nd = len(shape)
    return pl.BlockSpec(shape, lambda *_: (0,) * nd, pipeline_mode=pl.Buffered(1))


def _rms(x, g):
    ms = jnp.mean(x * x, axis=-1, keepdims=True)
    return x * lax.rsqrt(ms + EPS) * g


def _dot(a, b):
    return jnp.dot(a, b, preferred_element_type=F32)


def _dot_nt(a, b):
    return lax.dot_general(a, b, (((1,), (1,)), ((), ())), preferred_element_type=F32)


def _dot_tn(a, b):
    return lax.dot_general(a, b, (((0,), (0,)), ((), ())), preferred_element_type=F32)


def _silu(x):
    return x * (1.0 / (1.0 + jnp.exp(-x)))


def _da_in_body(x_ref, g_ref, w_ref, qT_ref, k_ref, vT_ref, t_s, *, d, q_scale):
    h = _rms(x_ref[...], g_ref[...]).astype(BF16)
    t_s[...] = _dot(h, w_ref[:, 0:d]) * q_scale
    qT_ref[...] = t_s[...].T.astype(BF16)
    k_ref[...] = _dot(h, w_ref[:, d:2 * d]).astype(BF16)
    t_s[...] = _dot(h, w_ref[:, 2 * d:3 * d])
    vT_ref[...] = t_s[...].T.astype(BF16)


def _da_in(x, g, w, q_scale):
    b, s, d = x.shape
    tm = TOK_TILE
    return pl.pallas_call(
        functools.partial(_da_in_body, d=d, q_scale=q_scale),
        grid=(b, s // tm),
        in_specs=[
            pl.BlockSpec((None, tm, d), lambda i, j: (i, j, 0)),
            _resident((1, d)),
            _resident((d, 3 * d)),
        ],
        out_specs=[
            pl.BlockSpec((None, d, tm), lambda i, j: (i, 0, j)),
            pl.BlockSpec((None, tm, d), lambda i, j: (i, j, 0)),
            pl.BlockSpec((None, d, tm), lambda i, j: (i, 0, j)),
        ],
        out_shape=[
            jax.ShapeDtypeStruct((b, d, s), BF16),
            jax.ShapeDtypeStruct((b, s, d), BF16),
            jax.ShapeDtypeStruct((b, d, s), BF16),
        ],
        scratch_shapes=[pltpu.VMEM((tm, d), F32)],
        compiler_params=_params(("parallel", "parallel")),
        name="da_in",
    )(x, g, w)


def _rel_bucket_np(dist):
    n = np.maximum(dist, 0)
    exact = REL_BUCKETS // 2
    nf = np.maximum(n, exact).astype(np.float32)
    large = exact + (np.log(nf / np.float32(exact)) / np.float32(math.log(REL_MAX_DIST / exact))
                     * np.float32(REL_BUCKETS - exact)).astype(np.int32)
    large = np.minimum(large, REL_BUCKETS - 1)
    return np.where(n < exact, n, large).astype(np.int32)


def _bucket_tiles(t):
    kk = np.arange(t, dtype=np.int64)[:, None]
    qq = np.arange(t, dtype=np.int64)[None, :]
    d_diag = qq - kk
    diag = np.where(d_diag >= 0, _rel_bucket_np(d_diag), -1).astype(np.int32)
    off = _rel_bucket_np(qq + t - kk)
    assert int(_rel_bucket_np(np.arange(t + 1, 8 * t)).min()) == REL_BUCKETS - 1
    return diag, off


def _da_bias_body(rb_ref, bd_ref, bo_ref, out_ref, *, t):
    h = pl.program_id(0)
    far = rb_ref[h, REL_BUCKETS - 1]
    for which, bref in ((0, bd_ref), (1, bo_ref)):
        bk = bref[...]
        acc = jnp.zeros((t, t), F32)
        for bb in range(REL_BUCKETS - 1):
            acc = jnp.where(bk == bb, rb_ref[h, bb] - far, acc)
        acc = jnp.where(bk < 0, NEG_BIG, acc * LOG2E)
        out_ref[which, :, 0:t] = acc
        out_ref[which, :, t:2 * t] = acc


def _da_bias(rel_bias, t):
    diag, off = _bucket_tiles(t)
    hh = rel_bias.shape[0]
    return pl.pallas_call(
        functools.partial(_da_bias_body, t=t),
        grid=(hh,),
        in_specs=[
            pl.BlockSpec(memory_space=pltpu.SMEM),
            _resident((t, t)),
            _resident((t, t)),
        ],
        out_specs=pl.BlockSpec((None, 2, t, 2 * t), lambda i: (i, 0, 0, 0)),
        out_shape=jax.ShapeDtypeStruct((hh, 2, t, 2 * t), F32),
        compiler_params=_params(("parallel",)),
        name="da_bias",
    )(rel_bias, jnp.asarray(diag), jnp.asarray(off))


def _da_attn_body(lv_ref, sub_ref, qT_ref, k_ref, vT_ref, bias_ref, o_ref,
                  qb_s, m_s, l_s, acc_s, *, t, lam_init):
    qi = pl.program_id(2)
    qT = qT_ref[...]
    row = lax.broadcasted_iota(jnp.int32, qT.shape, 0)
    zero = jnp.zeros_like(qT)
    qb_s[:, 0:t] = jnp.where(row < DA_QK_DIM, qT, zero)
    qb_s[:, t:2 * t] = jnp.where(row >= DA_QK_DIM, qT, zero)
    m_s[...] = jnp.full(m_s.shape, NEG_BIG, F32)
    l_s[...] = jnp.zeros(l_s.shape, F32)
    acc_s[...] = jnp.zeros(acc_s.shape, F32)

    def tile(kj, bias):
        off = pl.multiple_of(kj * t, t)
        s = _dot(k_ref[pl.ds(off, t), :], qb_s[...])
        if bias is not None:
            s = s + bias
        m_old = m_s[...]
        m_new = jnp.maximum(m_old, jnp.max(s, axis=0, keepdims=True))
        alpha = jnp.exp2(m_old - m_new)
        p = jnp.exp2(s - m_new)
        l_s[...] = alpha * l_s[...] + jnp.sum(p, axis=0, keepdims=True)
        pv = _dot(vT_ref[:, pl.ds(off, t)], p.astype(BF16))
        acc_s[...] = alpha * acc_s[...] + pv
        m_s[...] = m_new

    def far_tile(kj, carry):
        tile(kj, None)
        return carry

    lax.fori_loop(0, jnp.maximum(qi - 1, 0), far_tile, 0)

    @pl.when(qi >= 1)
    def _():
        tile(qi - 1, bias_ref[1])

    tile(qi, bias_ref[0])

    lv = lv_ref[...]
    s1 = jnp.sum(lv[0:1] * lv[1:2], axis=-1, keepdims=True)
    s2 = jnp.sum(lv[2:3] * lv[3:4], axis=-1, keepdims=True)
    lam = jnp.exp(s1) - jnp.exp(s2) + lam_init
    o2 = acc_s[...] * (1.0 / l_s[...])
    oT = o2[:, 0:t] - lam * o2[:, t:2 * t]
    o = oT.T
    o = _rms(o, sub_ref[...]) * (1.0 - lam_init)
    o_ref[...] = o.astype(BF16)


def _da_attn(lv, subln, qT, k, vT, bias, lam_init):
    b, d, s = qT.shape
    t = ATT_TILE
    hd = DA_V_DIM
    nh = d // hd
    return pl.pallas_call(
        functools.partial(_da_attn_body, t=t, lam_init=lam_init),
        grid=(b, nh, s // t),
        in_specs=[
            _resident(lv.shape),
            _resident(subln.shape),
            pl.BlockSpec((None, hd, t), lambda i, h, j: (i, h, j)),
            pl.BlockSpec((None, s, hd), lambda i, h, j: (i, 0, h)),
            pl.BlockSpec((None, hd, s), lambda i, h, j: (i, h, 0)),
            pl.BlockSpec((None, 2, t, 2 * t), lambda i, h, j: (h, 0, 0, 0)),
        ],
        out_specs=pl.BlockSpec((None, t, hd), lambda i, h, j: (i, j, h)),
        out_shape=jax.ShapeDtypeStruct((b, s, d), BF16),
        scratch_shapes=[
            pltpu.VMEM((hd, 2 * t), BF16),
            pltpu.VMEM((1, 2 * t), F32),
            pltpu.VMEM((1, 2 * t), F32),
            pltpu.VMEM((hd, 2 * t), F32),
        ],
        compiler_params=_params(("parallel", "parallel", "arbitrary")),
        name="da_attn",
    )(lv, subln, qT, k, vT, bias)


def _hg_in_body(x_ref, g_ref, w_ref, lbs_ref, q_ref, k_ref, v_ref, gate_ref, lf_ref, *, d, layer):
    h = _rms(x_ref[...], g_ref[...]).astype(BF16)
    lbs = lbs_ref[...]
    e = jnp.exp(lbs - jnp.max(lbs, axis=0, keepdims=True))
    sm = e / jnp.sum(e, axis=0, keepdims=True)
    lb = jnp.sum(sm[0:layer + 1], axis=0, keepdims=True) - sm[0:1]

    q = _dot(h, w_ref[:, 0:d])
    q_ref[...] = _silu(q).astype(BF16)
    f = _dot(h, w_ref[:, d:2 * d])
    log_sig = -(jnp.maximum(-f, 0.0) + jnp.log(1.0 + jnp.exp(-jnp.abs(f))))
    a = jnp.log(lb)
    c = jnp.log(1.0 - lb) + log_sig
    log_f = jnp.maximum(a, c) + jnp.log(1.0 + jnp.exp(-jnp.abs(a - c)))
    lf_ref[...] = log_f
    k_ref[...] = (1.0 - jnp.exp(log_f)).astype(BF16)
    v_ref[...] = _dot(h, w_ref[:, 2 * d:3 * d]).astype(BF16)
    gate_ref[...] = _dot(h, w_ref[:, 3 * d:4 * d]).astype(BF16)


def _hg_in(x, g, w, lbs, layer):
    b, s, d = x.shape
    tm = TOK_TILE
    tok = pl.BlockSpec((None, tm, d), lambda i, j: (i, j, 0))
    return pl.pallas_call(
        functools.partial(_hg_in_body, d=d, layer=layer),
        grid=(b, s // tm),
        in_specs=[tok, _resident((1, d)), _resident((d, 4 * d)), _resident(lbs.shape)],
        out_specs=[tok, tok, tok, tok, tok],
        out_shape=[jax.ShapeDtypeStruct((b, s, d), BF16)] * 4 + [jax.ShapeDtypeStruct((b, s, d), F32)],
        compiler_params=_params(("parallel", "parallel")),
        name="hg_in",
    )(x, g, w, lbs)


def _hg_core_body(q_ref, k_ref, v_ref, gate_ref, lf_ref, on_ref, o_ref,
                  st_s, g_s, q_s, k_s, v_s, o_s, *, tile):
    c64, c16 = HG_CHUNK, HG_SUB

    @pl.when(pl.program_id(2) == 0)
    def _():
        st_s[...] = jnp.zeros(st_s.shape, F32)

    ri = lax.broadcasted_iota(jnp.int32, (c64, c64), 0)
    ci = lax.broadcasted_iota(jnp.int32, (c64, c64), 1)
    tri = jnp.where(ri >= ci, 1.0, 0.0).astype(BF16)
    tt = lax.broadcasted_iota(jnp.int32, (c16, 1), 0)

    def chunk(c, carry):
        base = pl.multiple_of(c * c64, c64)
        rows = pl.ds(base, c64)
        lf = lf_ref[rows, :]
        hi = lf.astype(BF16)
        r1 = lf - hi.astype(F32)
        mid = r1.astype(BF16)
        lo = (r1 - mid.astype(F32)).astype(BF16)
        gc = _dot(tri, hi) + _dot(tri, mid) + _dot(tri, lo)
        q = q_ref[rows, :].astype(F32)
        k = k_ref[rows, :].astype(F32)
        v = v_ref[rows, :].astype(F32)
        g_s[...] = gc
        q_s[...] = q
        k_s[...] = k
        v_s[...] = v
        g_last = gc[c64 - 1:c64, :]

        st = st_s[...]
        qg = (q * jnp.exp(gc)).astype(BF16)
        o_s[...] = _dot_nt(qg, st.astype(BF16))
        kd = (k * jnp.exp(g_last - gc)).astype(BF16)
        st_s[...] = st * jnp.exp(g_last) + _dot_tn(v.astype(BF16), kd)

        for i in range(1, c64 // c16):
            lo_r, hi_r = i * c16, (i + 1) * c16
            ref = g_s[lo_r - 1:lo_r, :]
            qi = (q_s[lo_r:hi_r, :] * jnp.exp(g_s[lo_r:hi_r, :] - ref)).astype(BF16)
            kj = (k_s[0:lo_r, :] * jnp.exp(ref - g_s[0:lo_r, :])).astype(BF16)
            a = _dot_nt(qi, kj)
            o_s[lo_r:hi_r, :] += _dot(a.astype(BF16), v_s[0:lo_r, :].astype(BF16))

        for i in range(c64 // c16):
            lo_r, hi_r = i * c16, (i + 1) * c16
            gi = g_s[lo_r:hi_r, :]
            qi = q_s[lo_r:hi_r, :]
            acc = o_s[lo_r:hi_r, :]
            for s in range(c16):
                j = lo_r + s
                e = jnp.exp(jnp.minimum(gi - g_s[j:j + 1, :], 0.0))
                a = jnp.sum(qi * k_s[j:j + 1, :] * e, axis=-1, keepdims=True)
                a = jnp.where(tt >= s, a, 0.0)
                acc = acc + a * v_s[j:j + 1, :]
            o_s[lo_r:hi_r, :] = acc

        o = _rms(o_s[...], on_ref[...])
        gate = gate_ref[rows, :].astype(F32)
        o_ref[rows, :] = (o * _silu(gate)).astype(BF16)
        return carry

    lax.fori_loop(0, tile // c64, chunk, 0)


def _hg_core(q, k, v, gate, lf, onorm):
    b, s, d = q.shape
    hd = HG_DIM
    tile = HG_TILE
    blk = pl.BlockSpec((None, tile, hd), lambda i, h, j: (i, j, h))
    return pl.pallas_call(
        functools.partial(_hg_core_body, tile=tile),
        grid=(b, d // hd, s // tile),
        in_specs=[blk, blk, blk, blk, blk, _resident(onorm.shape)],
        out_specs=blk,
        out_shape=jax.ShapeDtypeStruct((b, s, d), BF16),
        scratch_shapes=[
            pltpu.VMEM((hd, hd), F32),
            pltpu.VMEM((HG_CHUNK, hd), F32),
            pltpu.VMEM((HG_CHUNK, hd), F32),
            pltpu.VMEM((HG_CHUNK, hd), F32),
            pltpu.VMEM((HG_CHUNK, hd), F32),
            pltpu.VMEM((HG_CHUNK, hd), F32),
        ],
        compiler_params=_params(("parallel", "parallel", "arbitrary")),
        name="hg_core",
    )(q, k, v, gate, lf, onorm)


def _gelu(x):
    return 0.5 * x * (1.0 + lax.erf(x * (1.0 / math.sqrt(2.0))))


def _sgu_body(x_ref, g_ref, win_ref, vn_ref, ws_ref, bsT_ref, wout_ref, o_ref, a_s, *, d, tm):
    x = x_ref[...]
    h = _rms(x, g_ref[...]).astype(BF16)
    v = _gelu(_dot(h, win_ref[:, d:2 * d]))
    v = _rms(v, vn_ref[...]).astype(BF16)
    gd = d // SG_GROUPS
    ri = lax.broadcasted_iota(jnp.int32, (SG_CHUNK, SG_CHUNK), 0)
    ci = lax.broadcasted_iota(jnp.int32, (SG_CHUNK, SG_CHUNK), 1)
    bsT = bsT_ref[...]
    for pair in range(SG_GROUPS // 2):
        u2 = _gelu(_dot(h, win_ref[:, 2 * pair * gd:2 * (pair + 1) * gd]))
        for half in range(2):
            gidx = 2 * pair + half
            cols = slice(gidx * gd, (gidx + 1) * gd)
            w = jnp.where(ri >= ci, ws_ref[gidx], 0.0).astype(BF16)
            bias = bsT[:, gidx:gidx + 1]
            for c in range(tm // SG_CHUNK):
                rows = slice(c * SG_CHUNK, (c + 1) * SG_CHUNK)
                mixed = _dot(w, v[rows, cols]) + bias
                a_s[rows, cols] = (u2[rows, half * gd:(half + 1) * gd] * mixed).astype(BF16)
    o_ref[...] = x + _dot(a_s[...], wout_ref[...])


def _sgu(x, g, w_in, vnorm, w_s, b_sT, w_out):
    b, s, d = x.shape
    tm = TOK_TILE
    tok = pl.BlockSpec((None, tm, d), lambda i, j: (i, j, 0))
    return pl.pallas_call(
        functools.partial(_sgu_body, d=d, tm=tm),
        grid=(b, s // tm),
        in_specs=[tok, _resident((1, d)), _resident(w_in.shape), _resident((1, d)),
                  _resident(w_s.shape), _resident(b_sT.shape), _resident(w_out.shape)],
        out_specs=tok,
        out_shape=jax.ShapeDtypeStruct((b, s, d), F32),
        scratch_shapes=[pltpu.VMEM((tm, d), BF16)],
        compiler_params=_params(("parallel", "parallel")),
        name="sgu",
    )(x, g, w_in, vnorm, w_s, b_sT, w_out)


def _resproj_body(x_ref, a_ref, w_ref, o_ref):
    o_ref[...] = x_ref[...] + _dot(a_ref[...], w_ref[...])


def _resproj(x, a, w):
    b, s, d = x.shape
    tm = TOK_TILE
    tok = pl.BlockSpec((None, tm, d), lambda i, j: (i, j, 0))
    return pl.pallas_call(
        _resproj_body,
        grid=(b, s // tm),
        in_specs=[tok, tok, _resident(w.shape)],
        out_specs=tok,
        out_shape=jax.ShapeDtypeStruct((b, s, d), F32),
        compiler_params=_params(("parallel", "parallel")),
        name="resproj",
    )(x, a, w)


def _kv_mem_body(mem_ref, g_ref, w_ref, k_ref, v_ref, *, d):
    h = _rms(mem_ref[...], g_ref[...]).astype(BF16)
    k_ref[...] = _dot(h, w_ref[:, 0:d]).astype(BF16)
    v_ref[...] = _dot(h, w_ref[:, d:2 * d]).astype(BF16)


def _kv_mem(mem, g, w_kv):
    b, m, d = mem.shape
    depth = w_kv.shape[0]
    return pl.pallas_call(
        functools.partial(_kv_mem_body, d=d),
        grid=(depth, b),
        in_specs=[
            pl.BlockSpec((None, m, d), lambda l, i: (i, 0, 0)),
            _resident((1, d)),
            pl.BlockSpec((None, d, 2 * d), lambda l, i: (l, 0, 0)),
        ],
        out_specs=[
            pl.BlockSpec((None, None, m, d), lambda l, i: (l, i, 0, 0)),
            pl.BlockSpec((None, None, m, d), lambda l, i: (l, i, 0, 0)),
        ],
        out_shape=[
            jax.ShapeDtypeStruct((depth, b, m, d), BF16),
            jax.ShapeDtypeStruct((depth, b, m, d), BF16),
        ],
        compiler_params=_params(("parallel", "parallel")),
        name="kv_mem",
    )(mem, g, w_kv)


def _cross_body(x_ref, g_ref, wq_ref, k_ref, v_ref, wo_ref, o_ref, a_s, *, d, q_scale):
    x = x_ref[...]
    h = _rms(x, g_ref[...]).astype(BF16)
    hd = d // CA_HEADS
    for hh in range(CA_HEADS):
        cols = slice(hh * hd, (hh + 1) * hd)
        q = (_dot(h, wq_ref[:, cols]) * q_scale).astype(BF16)
        s = _dot_nt(q, k_ref[:, cols])
        m = jnp.max(s, axis=-1, keepdims=True)
        p = jnp.exp2(s - m)
        l = jnp.sum(p, axis=-1, keepdims=True)
        o = _dot(p.astype(BF16), v_ref[:, cols]) * (1.0 / l)
        a_s[:, cols] = o.astype(BF16)
    o_ref[...] = x + _dot(a_s[...], wo_ref[...])


def _cross(x, g, wq, k, v, wo, layer):
    b, s, d = x.shape
    m = v.shape[2]
    tm = TOK_TILE
    tok = pl.BlockSpec((None, tm, d), lambda i, j: (i, j, 0))
    q_scale = (d // CA_HEADS) ** -0.5 * LOG2E
    return pl.pallas_call(
        functools.partial(_cross_body, d=d, q_scale=q_scale),
        grid=(b, s // tm),
        in_specs=[
            tok, _resident((1, d)), _resident(wq.shape),
            pl.BlockSpec((None, None, m, d), lambda i, j: (layer, i, 0, 0)),
            pl.BlockSpec((None, None, m, d), lambda i, j: (layer, i, 0, 0)),
            _resident(wo.shape),
        ],
        out_specs=tok,
        out_shape=jax.ShapeDtypeStruct((b, s, d), F32),
        scratch_shapes=[pltpu.VMEM((tm, d), BF16)],
        compiler_params=_params(("parallel", "parallel")),
        name="cross",
    )(x, g, wq, k, v, wo)


def _ffn_body(x_ref, g_ref, wgu_ref, wd_ref, gf_ref, o_ref, a_s, *, ff, final_norm):
    x = x_ref[...]
    h = _rms(x, g_ref[...]).astype(BF16)
    fc = FFN_CHUNK
    for c in range(ff // fc):
        gate = _dot(h, wgu_ref[:, c * fc:(c + 1) * fc])
        up = _dot(h, wgu_ref[:, ff + c * fc:ff + (c + 1) * fc])
        a_s[:, c * fc:(c + 1) * fc] = (_silu(gate) * up).astype(BF16)
    y = x + _dot(a_s[...], wd_ref[...])
    if final_norm:
        y = _rms(y, gf_ref[...])
    o_ref[...] = y


def _ffn(x, g, wgu, wd, gf, final_norm):
    b, s, d = x.shape
    ff = wd.shape[0]
    tm = TOK_TILE
    tok = pl.BlockSpec((None, tm, d), lambda i, j: (i, j, 0))
    return pl.pallas_call(
        functools.partial(_ffn_body, ff=ff, final_norm=final_norm),
        grid=(b, s // tm),
        in_specs=[tok, _resident((1, d)), _resident(wgu.shape), _resident(wd.shape), _resident((1, d))],
        out_specs=tok,
        out_shape=jax.ShapeDtypeStruct((b, s, d), F32),
        scratch_shapes=[pltpu.VMEM((tm, ff), BF16)],
        compiler_params=_params(("parallel", "parallel")),
        name="ffn",
    )(x, g, wgu, wd, gf)


def kernel(x, mem, rel_bias, norm_mix, norm_cross, norm_ffn, norm_mem, norm_final,
           da_w_in, da_w_out, da_lq1, da_lk1, da_lq2, da_lk2, da_subln,
           hg_w_in, hg_w_out, hg_lower_bounds, hg_onorm,
           sg_w_in, sg_w_out, sg_vnorm, sg_w_s, sg_b_s,
           ca_w_q, ca_w_kv, ca_w_o, ffn_w_gu, ffn_w_down):
    depth = norm_mix.shape[0]
    row = lambda vec: vec.reshape(1, -1)
    bf = lambda w: w.astype(BF16)

    kT_mem, v_mem = _kv_mem(mem, row(norm_mem), bf(ca_w_kv))
    bias = _da_bias(rel_bias, ATT_TILE)
    da_scale = DA_QK_DIM ** -0.5 * LOG2E

    for i in range(depth):
        kind, j = i % N_MIXERS, i // N_MIXERS
        g_mix = row(norm_mix[i])
        if kind == 0:
            lam_init = 0.8 - 0.6 * math.exp(-0.3 * i)
            qT, k, vT = _da_in(x, g_mix, bf(da_w_in[j]), da_scale)
            lv = jnp.stack([da_lq1[j], da_lk1[j], da_lq2[j], da_lk2[j]])
            a = _da_attn(lv, row(da_subln[j]), qT, k, vT, bias, lam_init)
            x = _resproj(x, a, bf(da_w_out[j]))
        elif kind == 1:
            q, k, v, gate, lf = _hg_in(x, g_mix, bf(hg_w_in[j]), hg_lower_bounds, i)
            a = _hg_core(q, k, v, gate, lf, row(hg_onorm[j]))
            x = _resproj(x, a, bf(hg_w_out[j]))
        else:
            x = _sgu(x, g_mix, bf(sg_w_in[j]), row(sg_vnorm[j]), sg_w_s[j], sg_b_s[j].T, bf(sg_w_out[j]))
        x = _cross(x, row(norm_cross[i]), bf(ca_w_q[i]), kT_mem, v_mem, bf(ca_w_o[i]), i)
        x = _ffn(x, row(norm_ffn[i]), bf(ffn_w_gu[i]), bf(ffn_w_down[i]), row(norm_final), i == depth - 1)
    return x
```

```python
import functools
import math

import numpy as np
import jax
import jax.numpy as jnp
from jax import lax
from jax.experimental import pallas as pl
from jax.experimental.pallas import tpu as pltpu

F32 = jnp.float32
BF16 = jnp.bfloat16

EPS = 1e-6
LOG2E = math.log2(math.e)
NEG_BIG = -1e30

DA_HEADS = 8
DA_QK_DIM = 64
DA_V_DIM = 128
REL_BUCKETS = 32
REL_MAX_DIST = 128
HG_HEADS = 8
HG_DIM = 128
HG_CHUNK = 64
HG_SUB = 16
SG_CHUNK = 128
SG_GROUPS = 8
CA_HEADS = 4
N_MIXERS = 3

VMEM_LIMIT = 56 * 1024 * 1024
TOK_TILE = 512
ATT_TILE = 512
ATT_STRIP = 256
HG_TILE = 512
FFN_CHUNK = 256


def _params(sem):
    return pltpu.CompilerParams(dimension_semantics=sem, vmem_limit_bytes=VMEM_LIMIT)


def _resident(shape):
    nd = len(shape)
    return pl.BlockSpec(shape, lambda *_: (0,) * nd, pipeline_mode=pl.Buffered(1))


def _rms(x, g):
    ms = jnp.mean(x * x, axis=-1, keepdims=True)
    return x * lax.rsqrt(ms + EPS) * g


def _dot(a, b):
    return jnp.dot(a, b, preferred_element_type=F32)


def _dot_nt(a, b):
    return lax.dot_general(a, b, (((1,), (1,)), ((), ())), preferred_element_type=F32)


def _dot_tn(a, b):
    return lax.dot_general(a, b, (((0,), (0,)), ((), ())), preferred_element_type=F32)


def _silu(x):
    return x * (1.0 / (1.0 + jnp.exp(-x)))


def _da_in_body(x_ref, g_ref, w_ref, qT_ref, k_ref, vT_ref, t_s, *, d, q_scale):
    h = _rms(x_ref[...], g_ref[...]).astype(BF16)
    t_s[...] = _dot(h, w_ref[:, 0:d]) * q_scale
    qT_ref[...] = t_s[...].T.astype(BF16)
    k_ref[...] = _dot(h, w_ref[:, d:2 * d]).astype(BF16)
    t_s[...] = _dot(h, w_ref[:, 2 * d:3 * d])
    vT_ref[...] = t_s[...].T.astype(BF16)


def _da_in(x, g, w, q_scale):
    b, s, d = x.shape
    tm = TOK_TILE
    return pl.pallas_call(
        functools.partial(_da_in_body, d=d, q_scale=q_scale),
        grid=(b, s // tm),
        in_specs=[
            pl.BlockSpec((None, tm, d), lambda i, j: (i, j, 0)),
            _resident((1, d)),
            _resident((d, 3 * d)),
        ],
        out_specs=[
            pl.BlockSpec((None, d, tm), lambda i, j: (i, 0, j)),
            pl.BlockSpec((None, tm, d), lambda i, j: (i, j, 0)),
            pl.BlockSpec((None, d, tm), lambda i, j: (i, 0, j)),
        ],
        out_shape=[
            jax.ShapeDtypeStruct((b, d, s), BF16),
            jax.ShapeDtypeStruct((b, s, d), BF16),
            jax.ShapeDtypeStruct((b, d, s), BF16),
        ],
        scratch_shapes=[pltpu.VMEM((tm, d), F32)],
        compiler_params=_params(("parallel", "parallel")),
        name="da_in",
    )(x, g, w)


def _rel_bucket_np(dist):
    n = np.maximum(dist, 0)
    exact = REL_BUCKETS // 2
    nf = np.maximum(n, exact).astype(np.float32)
    large = exact + (np.log(nf / np.float32(exact)) / np.float32(math.log(REL_MAX_DIST / exact))
                     * np.float32(REL_BUCKETS - exact)).astype(np.int32)
    large = np.minimum(large, REL_BUCKETS - 1)
    return np.where(n < exact, n, large).astype(np.int32)


def _bucket_tiles(t):
    kk = np.arange(t, dtype=np.int64)[:, None]
    qq = np.arange(t, dtype=np.int64)[None, :]
    d_diag = qq - kk
    diag = np.where(d_diag >= 0, _rel_bucket_np(d_diag), -1).astype(np.int32)
    off = _rel_bucket_np(qq + t - kk)
    assert int(_rel_bucket_np(np.arange(t + 1, 8 * t)).min()) == REL_BUCKETS - 1
    return diag, off


def _da_bias_body(rb_ref, bd_ref, bo_ref, out_ref, *, t):
    h = pl.program_id(0)
    far = rb_ref[h, REL_BUCKETS - 1]
    for which, bref in ((0, bd_ref), (1, bo_ref)):
        bk = bref[...]
        acc = jnp.zeros((t, t), F32)
        for bb in range(REL_BUCKETS - 1):
            acc = jnp.where(bk == bb, rb_ref[h, bb] - far, acc)
        out_ref[which] = jnp.where(bk < 0, NEG_BIG, acc * LOG2E)


def _da_bias(rel_bias, t):
    diag, off = _bucket_tiles(t)
    hh = rel_bias.shape[0]
    return pl.pallas_call(
        functools.partial(_da_bias_body, t=t),
        grid=(hh,),
        in_specs=[
            pl.BlockSpec(memory_space=pltpu.SMEM),
            _resident((t, t)),
            _resident((t, t)),
        ],
        out_specs=pl.BlockSpec((None, 2, t, t), lambda i: (i, 0, 0, 0)),
        out_shape=jax.ShapeDtypeStruct((hh, 2, t, t), F32),
        compiler_params=_params(("parallel",)),
        name="da_bias",
    )(rel_bias, jnp.asarray(diag), jnp.asarray(off))


def _da_attn_body(lv_ref, sub_ref, qT_ref, k_ref, vT_ref, bias_ref, o_ref,
                  qb_s, m_s, l_s, acc_s, s0_s, s1_s, *, tq, tk, lam_init):
    qi = pl.program_id(2)
    sw = ATT_STRIP
    per_half = tk // sw
    qT = qT_ref[...]
    row = lax.broadcasted_iota(jnp.int32, qT.shape, 0)
    zero = jnp.zeros_like(qT)
    qb_s[:, 0:tq] = jnp.where(row < DA_QK_DIM, qT, zero)
    qb_s[:, tq:2 * tq] = jnp.where(row >= DA_QK_DIM, qT, zero)
    m_s[...] = jnp.full(m_s.shape, NEG_BIG, F32)
    l_s[...] = jnp.zeros(l_s.shape, F32)
    acc_s[...] = jnp.zeros(acc_s.shape, F32)

    kinds = {"far": (None, None), "pre": (1, None), "d0": (0, 1), "d1": ("skip", 0)}

    def strips(kind):
        for comp in range(2):
            for r in range(tq // sw):
                which = kinds[kind][r // per_half]
                if which == "skip":
                    continue
                yield slice(comp * tq + r * sw, comp * tq + (r + 1) * sw), which, (r % per_half) * sw

    def stage_a(kj, kind, buf):
        kt = k_ref[pl.ds(pl.multiple_of(kj * tk, tk), tk), :]
        for cols, which, qoff in strips(kind):
            s = _dot(kt, qb_s[:, cols])
            if which is not None:
                s = s + bias_ref[which, :, qoff:qoff + sw]
            buf[:, cols] = s

    def stage_b(kj, kind, buf):
        vt = vT_ref[:, pl.ds(pl.multiple_of(kj * tk, tk), tk)]
        for cols, _, _ in strips(kind):
            s = buf[:, cols]
            m_old = m_s[:, cols]
            m_new = jnp.maximum(m_old, jnp.max(s, axis=0, keepdims=True))
            alpha = jnp.exp2(m_old - m_new)
            p = jnp.exp2(s - m_new)
            l_s[:, cols] = alpha * l_s[:, cols] + jnp.sum(p, axis=0, keepdims=True)
            acc_s[:, cols] = alpha * acc_s[:, cols] + _dot(vt, p.astype(BF16))
            m_s[:, cols] = m_new

    @pl.when(qi == 0)
    def _():
        stage_a(0, "d0", s0_s)
        stage_a(1, "d1", s1_s)
        stage_b(0, "d0", s0_s)
        stage_b(1, "d1", s1_s)

    @pl.when(qi > 0)
    def _():
        stage_a(0, "far", s0_s)

        def far_pair(i, carry):
            stage_a(2 * i + 1, "far", s1_s)
            stage_b(2 * i, "far", s0_s)
            stage_a(2 * i + 2, "far", s0_s)
            stage_b(2 * i + 1, "far", s1_s)
            return carry

        lax.fori_loop(0, qi - 1, far_pair, 0)
        stage_a(2 * qi - 1, "pre", s1_s)
        stage_b(2 * qi - 2, "far", s0_s)
        stage_a(2 * qi, "d0", s0_s)
        stage_b(2 * qi - 1, "pre", s1_s)
        stage_a(2 * qi + 1, "d1", s1_s)
        stage_b(2 * qi, "d0", s0_s)
        stage_b(2 * qi + 1, "d1", s1_s)

    lv = lv_ref[...]
    s1 = jnp.sum(lv[0:1] * lv[1:2], axis=-1, keepdims=True)
    s2 = jnp.sum(lv[2:3] * lv[3:4], axis=-1, keepdims=True)
    lam = jnp.exp(s1) - jnp.exp(s2) + lam_init
    o2 = acc_s[...] * (1.0 / l_s[...])
    oT = o2[:, 0:tq] - lam * o2[:, tq:2 * tq]
    o = oT.T
    o = _rms(o, sub_ref[...]) * (1.0 - lam_init)
    o_ref[...] = o.astype(BF16)


def _da_attn(lv, subln, qT, k, vT, bias, lam_init):
    b, d, s = qT.shape
    tq, tk = 2 * ATT_TILE, ATT_TILE
    hd = DA_V_DIM
    nh = d // hd
    return pl.pallas_call(
        functools.partial(_da_attn_body, tq=tq, tk=tk, lam_init=lam_init),
        grid=(b, nh, s // tq),
        in_specs=[
            _resident(lv.shape),
            _resident(subln.shape),
            pl.BlockSpec((None, hd, tq), lambda i, h, j: (i, h, j)),
            pl.BlockSpec((None, s, hd), lambda i, h, j: (i, 0, h)),
            pl.BlockSpec((None, hd, s), lambda i, h, j: (i, h, 0)),
            pl.BlockSpec((None, 2, tk, tk), lambda i, h, j: (h, 0, 0, 0)),
        ],
        out_specs=pl.BlockSpec((None, tq, hd), lambda i, h, j: (i, j, h)),
        out_shape=jax.ShapeDtypeStruct((b, s, d), BF16),
        scratch_shapes=[
            pltpu.VMEM((hd, 2 * tq), BF16),
            pltpu.VMEM((1, 2 * tq), F32),
            pltpu.VMEM((1, 2 * tq), F32),
            pltpu.VMEM((hd, 2 * tq), F32),
            pltpu.VMEM((tk, 2 * tq), F32),
            pltpu.VMEM((tk, 2 * tq), F32),
        ],
        compiler_params=_params(("parallel", "parallel", "arbitrary")),
        name="da_attn",
    )(lv, subln, qT, k, vT, bias)


def _hg_in_body(x_ref, g_ref, w_ref, lbs_ref, q_ref, k_ref, v_ref, gate_ref, lf_ref, *, d, layer):
    h = _rms(x_ref[...], g_ref[...]).astype(BF16)
    lbs = lbs_ref[...]
    e = jnp.exp(lbs - jnp.max(lbs, axis=0, keepdims=True))
    sm = e / jnp.sum(e, axis=0, keepdims=True)
    lb = jnp.sum(sm[0:layer + 1], axis=0, keepdims=True) - sm[0:1]

    q = _dot(h, w_ref[:, 0:d])
    q_ref[...] = _silu(q).astype(BF16)
    f = _dot(h, w_ref[:, d:2 * d])
    log_sig = -(jnp.maximum(-f, 0.0) + jnp.log(1.0 + jnp.exp(-jnp.abs(f))))
    a = jnp.log(lb)
    c = jnp.log(1.0 - lb) + log_sig
    log_f = jnp.maximum(a, c) + jnp.log(1.0 + jnp.exp(-jnp.abs(a - c)))
    lf_ref[...] = log_f
    k_ref[...] = (1.0 - jnp.exp(log_f)).astype(BF16)
    v_ref[...] = _dot(h, w_ref[:, 2 * d:3 * d]).astype(BF16)
    gate_ref[...] = _dot(h, w_ref[:, 3 * d:4 * d]).astype(BF16)


def _hg_in(x, g, w, lbs, layer):
    b, s, d = x.shape
    tm = TOK_TILE
    tok = pl.BlockSpec((None, tm, d), lambda i, j: (i, j, 0))
    return pl.pallas_call(
        functools.partial(_hg_in_body, d=d, layer=layer),
        grid=(b, s // tm),
        in_specs=[tok, _resident((1, d)), _resident((d, 4 * d)), _resident(lbs.shape)],
        out_specs=[tok, tok, tok, tok, tok],
        out_shape=[jax.ShapeDtypeStruct((b, s, d), BF16)] * 4 + [jax.ShapeDtypeStruct((b, s, d), F32)],
        compiler_params=_params(("parallel", "parallel")),
        name="hg_in",
    )(x, g, w, lbs)


def _hg_core_body(q_ref, k_ref, v_ref, gate_ref, lf_ref, on_ref, o_ref,
                  st_s, g_s, q_s, k_s, v_s, o_s, *, tile):
    c64, c16 = HG_CHUNK, HG_SUB

    @pl.when(pl.program_id(2) == 0)
    def _():
        st_s[...] = jnp.zeros(st_s.shape, F32)

    ri = lax.broadcasted_iota(jnp.int32, (c64, c64), 0)
    ci = lax.broadcasted_iota(jnp.int32, (c64, c64), 1)
    tri = jnp.where(ri >= ci, 1.0, 0.0).astype(BF16)
    tt = lax.broadcasted_iota(jnp.int32, (c16, 1), 0)

    def chunk(c, carry):
        base = pl.multiple_of(c * c64, c64)
        rows = pl.ds(base, c64)
        lf = lf_ref[rows, :]
        hi = lf.astype(BF16)
        r1 = lf - hi.astype(F32)
        mid = r1.astype(BF16)
        lo = (r1 - mid.astype(F32)).astype(BF16)
        gc = _dot(tri, hi) + _dot(tri, mid) + _dot(tri, lo)
        q = q_ref[rows, :].astype(F32)
        k = k_ref[rows, :].astype(F32)
        v = v_ref[rows, :].astype(F32)
        g_s[...] = gc
        q_s[...] = q
        k_s[...] = k
        v_s[...] = v
        g_last = gc[c64 - 1:c64, :]

        st = st_s[...]
        qg = (q * jnp.exp(gc)).astype(BF16)
        o_s[...] = _dot_nt(qg, st.astype(BF16))
        kd = (k * jnp.exp(g_last - gc)).astype(BF16)
        st_s[...] = st * jnp.exp(g_last) + _dot_tn(v.astype(BF16), kd)

        for i in range(1, c64 // c16):
            lo_r, hi_r = i * c16, (i + 1) * c16
            ref = g_s[lo_r - 1:lo_r, :]
            qi = (q_s[lo_r:hi_r, :] * jnp.exp(g_s[lo_r:hi_r, :] - ref)).astype(BF16)
            kj = (k_s[0:lo_r, :] * jnp.exp(ref - g_s[0:lo_r, :])).astype(BF16)
            a = _dot_nt(qi, kj)
            o_s[lo_r:hi_r, :] += _dot(a.astype(BF16), v_s[0:lo_r, :].astype(BF16))

        for i in range(c64 // c16):
            lo_r, hi_r = i * c16, (i + 1) * c16
            gi = g_s[lo_r:hi_r, :]
            qi = q_s[lo_r:hi_r, :]
            acc = o_s[lo_r:hi_r, :]
            for s in range(c16):
                j = lo_r + s
                e = jnp.exp(jnp.minimum(gi - g_s[j:j + 1, :], 0.0))
                a = jnp.sum(qi * k_s[j:j + 1, :] * e, axis=-1, keepdims=True)
                a = jnp.where(tt >= s, a, 0.0)
                acc = acc + a * v_s[j:j + 1, :]
            o_s[lo_r:hi_r, :] = acc

        o = _rms(o_s[...], on_ref[...])
        gate = gate_ref[rows, :].astype(F32)
        o_ref[rows, :] = (o * _silu(gate)).astype(BF16)
        return carry

    lax.fori_loop(0, tile // c64, chunk, 0)


def _hg_core(q, k, v, gate, lf, onorm):
    b, s, d = q.shape
    hd = HG_DIM
    tile = HG_TILE
    blk = pl.BlockSpec((None, tile, hd), lambda i, h, j: (i, j, h))
    return pl.pallas_call(
        functools.partial(_hg_core_body, tile=tile),
        grid=(b, d // hd, s // tile),
        in_specs=[blk, blk, blk, blk, blk, _resident(onorm.shape)],
        out_specs=blk,
        out_shape=jax.ShapeDtypeStruct((b, s, d), BF16),
        scratch_shapes=[
            pltpu.VMEM((hd, hd), F32),
            pltpu.VMEM((HG_CHUNK, hd), F32),
            pltpu.VMEM((HG_CHUNK, hd), F32),
            pltpu.VMEM((HG_CHUNK, hd), F32),
            pltpu.VMEM((HG_CHUNK, hd), F32),
            pltpu.VMEM((HG_CHUNK, hd), F32),
        ],
        compiler_params=_params(("parallel", "parallel", "arbitrary")),
        name="hg_core",
    )(q, k, v, gate, lf, onorm)


def _gelu(x):
    return 0.5 * x * (1.0 + lax.erf(x * (1.0 / math.sqrt(2.0))))


def _sgu_body(x_ref, g_ref, win_ref, vn_ref, ws_ref, bsT_ref, wout_ref, o_ref, a_s, *, d, tm):
    x = x_ref[...]
    h = _rms(x, g_ref[...]).astype(BF16)
    v = _gelu(_dot(h, win_ref[:, d:2 * d]))
    v = _rms(v, vn_ref[...]).astype(BF16)
    gd = d // SG_GROUPS
    ri = lax.broadcasted_iota(jnp.int32, (SG_CHUNK, SG_CHUNK), 0)
    ci = lax.broadcasted_iota(jnp.int32, (SG_CHUNK, SG_CHUNK), 1)
    bsT = bsT_ref[...]
    for pair in range(SG_GROUPS // 2):
        u2 = _gelu(_dot(h, win_ref[:, 2 * pair * gd:2 * (pair + 1) * gd]))
        for half in range(2):
            gidx = 2 * pair + half
            cols = slice(gidx * gd, (gidx + 1) * gd)
            w = jnp.where(ri >= ci, ws_ref[gidx], 0.0).astype(BF16)
            bias = bsT[:, gidx:gidx + 1]
            for c in range(tm // SG_CHUNK):
                rows = slice(c * SG_CHUNK, (c + 1) * SG_CHUNK)
                mixed = _dot(w, v[rows, cols]) + bias
                a_s[rows, cols] = (u2[rows, half * gd:(half + 1) * gd] * mixed).astype(BF16)
    o_ref[...] = x + _dot(a_s[...], wout_ref[...])


def _sgu(x, g, w_in, vnorm, w_s, b_sT, w_out):
    b, s, d = x.shape
    tm = TOK_TILE
    tok = pl.BlockSpec((None, tm, d), lambda i, j: (i, j, 0))
    return pl.pallas_call(
        functools.partial(_sgu_body, d=d, tm=tm),
        grid=(b, s // tm),
        in_specs=[tok, _resident((1, d)), _resident(w_in.shape), _resident((1, d)),
                  _resident(w_s.shape), _resident(b_sT.shape), _resident(w_out.shape)],
        out_specs=tok,
        out_shape=jax.ShapeDtypeStruct((b, s, d), F32),
        scratch_shapes=[pltpu.VMEM((tm, d), BF16)],
        compiler_params=_params(("parallel", "parallel")),
        name="sgu",
    )(x, g, w_in, vnorm, w_s, b_sT, w_out)


def _resproj_body(x_ref, a_ref, w_ref, o_ref):
    o_ref[...] = x_ref[...] + _dot(a_ref[...], w_ref[...])


def _resproj(x, a, w):
    b, s, d = x.shape
    tm = TOK_TILE
    tok = pl.BlockSpec((None, tm, d), lambda i, j: (i, j, 0))
    return pl.pallas_call(
        _resproj_body,
        grid=(b, s // tm),
        in_specs=[tok, tok, _resident(w.shape)],
        out_specs=tok,
        out_shape=jax.ShapeDtypeStruct((b, s, d), F32),
        compiler_params=_params(("parallel", "parallel")),
        name="resproj",
    )(x, a, w)


def _kv_mem_body(mem_ref, g_ref, w_ref, k_ref, v_ref, *, d):
    h = _rms(mem_ref[...], g_ref[...]).astype(BF16)
    k_ref[...] = _dot(h, w_ref[:, 0:d]).astype(BF16)
    v_ref[...] = _dot(h, w_ref[:, d:2 * d]).astype(BF16)


def _kv_mem(mem, g, w_kv):
    b, m, d = mem.shape
    depth = w_kv.shape[0]
    return pl.pallas_call(
        functools.partial(_kv_mem_body, d=d),
        grid=(depth, b),
        in_specs=[
            pl.BlockSpec((None, m, d), lambda l, i: (i, 0, 0)),
            _resident((1, d)),
            pl.BlockSpec((None, d, 2 * d), lambda l, i: (l, 0, 0)),
        ],
        out_specs=[
            pl.BlockSpec((None, None, m, d), lambda l, i: (l, i, 0, 0)),
            pl.BlockSpec((None, None, m, d), lambda l, i: (l, i, 0, 0)),
        ],
        out_shape=[
            jax.ShapeDtypeStruct((depth, b, m, d), BF16),
            jax.ShapeDtypeStruct((depth, b, m, d), BF16),
        ],
        compiler_params=_params(("parallel", "parallel")),
        name="kv_mem",
    )(mem, g, w_kv)


def _cross_body(x_ref, g_ref, wq_ref, k_ref, v_ref, wo_ref, o_ref, a_s, *, d, q_scale):
    x = x_ref[...]
    h = _rms(x, g_ref[...]).astype(BF16)
    hd = d // CA_HEADS
    for hh in range(CA_HEADS):
        cols = slice(hh * hd, (hh + 1) * hd)
        q = (_dot(h, wq_ref[:, cols]) * q_scale).astype(BF16)
        s = _dot_nt(q, k_ref[:, cols])
        m = jnp.max(s, axis=-1, keepdims=True)
        p = jnp.exp2(s - m)
        l = jnp.sum(p, axis=-1, keepdims=True)
        o = _dot(p.astype(BF16), v_ref[:, cols]) * (1.0 / l)
        a_s[:, cols] = o.astype(BF16)
    o_ref[...] = x + _dot(a_s[...], wo_ref[...])


def _cross(x, g, wq, k, v, wo, layer):
    b, s, d = x.shape
    m = v.shape[2]
    tm = TOK_TILE
    tok = pl.BlockSpec((None, tm, d), lambda i, j: (i, j, 0))
    q_scale = (d // CA_HEADS) ** -0.5 * LOG2E
    return pl.pallas_call(
        functools.partial(_cross_body, d=d, q_scale=q_scale),
        grid=(b, s // tm),
        in_specs=[
            tok, _resident((1, d)), _resident(wq.shape),
            pl.BlockSpec((None, None, m, d), lambda i, j: (layer, i, 0, 0)),
            pl.BlockSpec((None, None, m, d), lambda i, j: (layer, i, 0, 0)),
            _resident(wo.shape),
        ],
        out_specs=tok,
        out_shape=jax.ShapeDtypeStruct((b, s, d), F32),
        scratch_shapes=[pltpu.VMEM((tm, d), BF16)],
        compiler_params=_params(("parallel", "parallel")),
        name="cross",
    )(x, g, wq, k, v, wo)


def _ffn_body(x_ref, g_ref, wgu_ref, wd_ref, gf_ref, o_ref, a_s, *, ff, final_norm):
    x = x_ref[...]
    h = _rms(x, g_ref[...]).astype(BF16)
    fc = FFN_CHUNK
    for c in range(ff // fc):
        gate = _dot(h, wgu_ref[:, c * fc:(c + 1) * fc])
        up = _dot(h, wgu_ref[:, ff + c * fc:ff + (c + 1) * fc])
        a_s[:, c * fc:(c + 1) * fc] = (_silu(gate) * up).astype(BF16)
    y = x + _dot(a_s[...], wd_ref[...])
    if final_norm:
        y = _rms(y, gf_ref[...])
    o_ref[...] = y


def _ffn(x, g, wgu, wd, gf, final_norm):
    b, s, d = x.shape
    ff = wd.shape[0]
    tm = TOK_TILE
    tok = pl.BlockSpec((None, tm, d), lambda i, j: (i, j, 0))
    return pl.pallas_call(
        functools.partial(_ffn_body, ff=ff, final_norm=final_norm),
        grid=(b, s // tm),
        in_specs=[tok, _resident((1, d)), _resident(wgu.shape), _resident(wd.shape), _resident((1, d))],
        out_specs=tok,
        out_shape=jax.ShapeDtypeStruct((b, s, d), F32),
        scratch_shapes=[pltpu.VMEM((tm, ff), BF16)],
        compiler_params=_params(("parallel", "parallel")),
        name="ffn",
    )(x, g, wgu, wd, gf)


def kernel(x, mem, rel_bias, norm_mix, norm_cross, norm_ffn, norm_mem, norm_final,
           da_w_in, da_w_out, da_lq1, da_lk1, da_lq2, da_lk2, da_subln,
           hg_w_in, hg_w_out, hg_lower_bounds, hg_onorm,
           sg_w_in, sg_w_out, sg_vnorm, sg_w_s, sg_b_s,
           ca_w_q, ca_w_kv, ca_w_o, ffn_w_gu, ffn_w_down):
    depth = norm_mix.shape[0]
    row = lambda vec: vec.reshape(1, -1)
    bf = lambda w: w.astype(BF16)

    kT_mem, v_mem = _kv_mem(mem, row(norm_mem), bf(ca_w_kv))
    bias = _da_bias(rel_bias, ATT_TILE)
    da_scale = DA_QK_DIM ** -0.5 * LOG2E

    for i in range(depth):
        kind, j = i % N_MIXERS, i // N_MIXERS
        g_mix = row(norm_mix[i])
        if kind == 0:
            lam_init = 0.8 - 0.6 * math.exp(-0.3 * i)
            qT, k, vT = _da_in(x, g_mix, bf(da_w_in[j]), da_scale)
            lv = jnp.stack([da_lq1[j], da_lk1[j], da_lq2[j], da_lk2[j]])
            a = _da_attn(lv, row(da_subln[j]), qT, k, vT, bias, lam_init)
            x = _resproj(x, a, bf(da_w_out[j]))
        elif kind == 1:
            q, k, v, gate, lf = _hg_in(x, g_mix, bf(hg_w_in[j]), hg_lower_bounds, i)
            a = _hg_core(q, k, v, gate, lf, row(hg_onorm[j]))
            x = _resproj(x, a, bf(hg_w_out[j]))
        else:
            x = _sgu(x, g_mix, bf(sg_w_in[j]), row(sg_vnorm[j]), sg_w_s[j], sg_b_s[j].T, bf(sg_w_out[j]))
        x = _cross(x, row(norm_cross[i]), bf(ca_w_q[i]), kT_mem, v_mem, bf(ca_w_o[i]), i)
        x = _ffn(x, row(norm_ffn[i]), bf(ffn_w_gu[i]), bf(ffn_w_down[i]), row(norm_final), i == depth - 1)
    return x
```

```python
import functools
import math

import numpy as np
import jax
import jax.numpy as jnp
from jax import lax
from jax.experimental import pallas as pl
from jax.experimental.pallas import tpu as pltpu

F32 = jnp.float32
BF16 = jnp.bfloat16

EPS = 1e-6
LOG2E = math.log2(math.e)
NEG_BIG = -1e30

DA_HEADS = 8
DA_QK_DIM = 64
DA_V_DIM = 128
REL_BUCKETS = 32
REL_MAX_DIST = 128
HG_HEADS = 8
HG_DIM = 128
HG_CHUNK = 64
HG_SUB = 16
HG_MAX_EXPONENT = 80.0
SG_CHUNK = 128
SG_GROUPS = 8
CA_HEADS = 4
N_MIXERS = 3

VMEM_LIMIT = 56 * 1024 * 1024
TOK_TILE = 512
ATT_TILE = 512
ATT_STRIP = 256
HG_TILE = 512
FFN_CHUNK = 256
CA_ROWS = 256
HG_IN_ROWS = 256


def _params(sem):
    return pltpu.CompilerParams(dimension_semantics=sem, vmem_limit_bytes=VMEM_LIMIT)


def _resident(shape):
    nd = len(shape)
    return pl.BlockSpec(shape, lambda *_: (0,) * nd, pipeline_mode=pl.Buffered(1))


def _rms(x, g):
    ms = jnp.mean(x * x, axis=-1, keepdims=True)
    return x * lax.rsqrt(ms + EPS) * g


def _dot(a, b):
    return jnp.dot(a, b, preferred_element_type=F32)


def _dot_nt(a, b):
    return lax.dot_general(a, b, (((1,), (1,)), ((), ())), preferred_element_type=F32)


def _dot_tn(a, b):
    return lax.dot_general(a, b, (((0,), (0,)), ((), ())), preferred_element_type=F32)


def _silu(x):
    return x * (1.0 / (1.0 + jnp.exp(-x)))


def _da_in_body(x_ref, g_ref, w_ref, qT_ref, k_ref, vT_ref, t_s, *, d, q_scale):
    h = _rms(x_ref[...], g_ref[...]).astype(BF16)
    t_s[...] = _dot(h, w_ref[:, 0:d]) * q_scale
    qT_ref[...] = t_s[...].T.astype(BF16)
    k_ref[...] = _dot(h, w_ref[:, d:2 * d]).astype(BF16)
    t_s[...] = _dot(h, w_ref[:, 2 * d:3 * d])
    vT_ref[...] = t_s[...].T.astype(BF16)


def _da_in(x, g, w, q_scale):
    b, s, d = x.shape
    tm = TOK_TILE
    return pl.pallas_call(
        functools.partial(_da_in_body, d=d, q_scale=q_scale),
        grid=(b, s // tm),
        in_specs=[
            pl.BlockSpec((None, tm, d), lambda i, j: (i, j, 0)),
            _resident((1, d)),
            _resident((d, 3 * d)),
        ],
        out_specs=[
            pl.BlockSpec((None, d, tm), lambda i, j: (i, 0, j)),
            pl.BlockSpec((None, tm, d), lambda i, j: (i, j, 0)),
            pl.BlockSpec((None, d, tm), lambda i, j: (i, 0, j)),
        ],
        out_shape=[
            jax.ShapeDtypeStruct((b, d, s), BF16),
            jax.ShapeDtypeStruct((b, s, d), BF16),
            jax.ShapeDtypeStruct((b, d, s), BF16),
        ],
        scratch_shapes=[pltpu.VMEM((tm, d), F32)],
        compiler_params=_params(("parallel", "parallel")),
        name="da_in",
    )(x, g, w)


def _rel_bucket_np(dist):
    n = np.maximum(dist, 0)
    exact = REL_BUCKETS // 2
    nf = np.maximum(n, exact).astype(np.float32)
    large = exact + (np.log(nf / np.float32(exact)) / np.float32(math.log(REL_MAX_DIST / exact))
                     * np.float32(REL_BUCKETS - exact)).astype(np.int32)
    large = np.minimum(large, REL_BUCKETS - 1)
    return np.where(n < exact, n, large).astype(np.int32)


def _bucket_tiles(t):
    kk = np.arange(t, dtype=np.int64)[:, None]
    qq = np.arange(t, dtype=np.int64)[None, :]
    d_diag = qq - kk
    diag = np.where(d_diag >= 0, _rel_bucket_np(d_diag), -1).astype(np.int32)
    off = _rel_bucket_np(qq + t - kk)
    assert int(_rel_bucket_np(np.arange(t + 1, 8 * t)).min()) == REL_BUCKETS - 1
    return diag, off


def _da_bias_body(rb_ref, bd_ref, bo_ref, out_ref, *, t):
    h = pl.program_id(0)
    far = rb_ref[h, REL_BUCKETS - 1]
    for which, bref in ((0, bd_ref), (1, bo_ref)):
        bk = bref[...]
        acc = jnp.zeros((t, t), F32)
        for bb in range(REL_BUCKETS - 1):
            acc = jnp.where(bk == bb, rb_ref[h, bb] - far, acc)
        out_ref[which] = jnp.where(bk < 0, NEG_BIG, acc * LOG2E)


def _da_bias(rel_bias, t):
    diag, off = _bucket_tiles(t)
    hh = rel_bias.shape[0]
    return pl.pallas_call(
        functools.partial(_da_bias_body, t=t),
        grid=(hh,),
        in_specs=[
            pl.BlockSpec(memory_space=pltpu.SMEM),
            _resident((t, t)),
            _resident((t, t)),
        ],
        out_specs=pl.BlockSpec((None, 2, t, t), lambda i: (i, 0, 0, 0)),
        out_shape=jax.ShapeDtypeStruct((hh, 2, t, t), F32),
        compiler_params=_params(("parallel",)),
        name="da_bias",
    )(rel_bias, jnp.asarray(diag), jnp.asarray(off))


def _da_attn_body(lv_ref, sub_ref, qT_ref, k_ref, vT_ref, bias_ref, o_ref,
                  qb_s, m_s, l_s, acc_s, s0_s, s1_s, mx0_s, mx1_s, *, tq, tk, lam_init):
    qi = pl.program_id(2)
    sw = ATT_STRIP
    per_half = tk // sw
    qT = qT_ref[...]
    row = lax.broadcasted_iota(jnp.int32, qT.shape, 0)
    zero = jnp.zeros_like(qT)
    qb_s[:, 0:tq] = jnp.where(row < DA_QK_DIM, qT, zero)
    qb_s[:, tq:2 * tq] = jnp.where(row >= DA_QK_DIM, qT, zero)
    m_s[...] = jnp.full(m_s.shape, NEG_BIG, F32)
    l_s[...] = jnp.zeros(l_s.shape, F32)
    acc_s[...] = jnp.zeros(acc_s.shape, F32)

    kinds = {"far": (None, None), "pre": (1, None), "d0": (0, 1), "d1": ("skip", 0)}

    def strips(kind):
        for comp in range(2):
            for r in range(tq // sw):
                which = kinds[kind][r // per_half]
                if which == "skip":
                    continue
                yield slice(comp * tq + r * sw, comp * tq + (r + 1) * sw), which, (r % per_half) * sw

    def stage_a(kj, kind, bufs):
        buf, mx = bufs
        kt = k_ref[pl.ds(pl.multiple_of(kj * tk, tk), tk), :]
        for cols, which, qoff in strips(kind):
            s = _dot(kt, qb_s[:, cols])
            if which is not None:
                s = s + bias_ref[which, :, qoff:qoff + sw]
            buf[:, cols] = s
            mx[:, cols] = jnp.max(s, axis=0, keepdims=True)

    def stage_b(kj, kind, bufs):
        buf, mx = bufs
        vt = vT_ref[:, pl.ds(pl.multiple_of(kj * tk, tk), tk)]
        for cols, _, _ in strips(kind):
            s = buf[:, cols]
            m_old = m_s[:, cols]
            m_new = jnp.maximum(m_old, mx[:, cols])
            alpha = jnp.exp2(m_old - m_new)
            p = jnp.exp2(s - m_new)
            l_s[:, cols] = alpha * l_s[:, cols] + jnp.sum(p, axis=0, keepdims=True)
            acc_s[:, cols] = alpha * acc_s[:, cols] + _dot(vt, p.astype(BF16))
            m_s[:, cols] = m_new

    b0, b1 = (s0_s, mx0_s), (s1_s, mx1_s)

    @pl.when(qi == 0)
    def _():
        stage_a(0, "d0", b0)
        stage_a(1, "d1", b1)
        stage_b(0, "d0", b0)
        stage_b(1, "d1", b1)

    @pl.when(qi > 0)
    def _():
        stage_a(0, "far", b0)

        def far_pair(i, carry):
            stage_a(2 * i + 1, "far", b1)
            stage_b(2 * i, "far", b0)
            stage_a(2 * i + 2, "far", b0)
            stage_b(2 * i + 1, "far", b1)
            return carry

        lax.fori_loop(0, qi - 1, far_pair, 0)
        stage_a(2 * qi - 1, "pre", b1)
        stage_b(2 * qi - 2, "far", b0)
        stage_a(2 * qi, "d0", b0)
        stage_b(2 * qi - 1, "pre", b1)
        stage_a(2 * qi + 1, "d1", b1)
        stage_b(2 * qi, "d0", b0)
        stage_b(2 * qi + 1, "d1", b1)

    lv = lv_ref[...]
    s1 = jnp.sum(lv[0:1] * lv[1:2], axis=-1, keepdims=True)
    s2 = jnp.sum(lv[2:3] * lv[3:4], axis=-1, keepdims=True)
    lam = jnp.exp(s1) - jnp.exp(s2) + lam_init
    o2 = acc_s[...] * (1.0 / l_s[...])
    oT = o2[:, 0:tq] - lam * o2[:, tq:2 * tq]
    o = oT.T
    o = _rms(o, sub_ref[...]) * (1.0 - lam_init)
    o_ref[...] = o.astype(BF16)


def _da_attn(lv, subln, qT, k, vT, bias, lam_init):
    b, d, s = qT.shape
    tq, tk = 2 * ATT_TILE, ATT_TILE
    hd = DA_V_DIM
    nh = d // hd
    return pl.pallas_call(
        functools.partial(_da_attn_body, tq=tq, tk=tk, lam_init=lam_init),
        grid=(b, nh, s // tq),
        in_specs=[
            _resident(lv.shape),
            _resident(subln.shape),
            pl.BlockSpec((None, hd, tq), lambda i, h, j: (i, h, j)),
            pl.BlockSpec((None, s, hd), lambda i, h, j: (i, 0, h)),
            pl.BlockSpec((None, hd, s), lambda i, h, j: (i, h, 0)),
            pl.BlockSpec((None, 2, tk, tk), lambda i, h, j: (h, 0, 0, 0)),
        ],
        out_specs=pl.BlockSpec((None, tq, hd), lambda i, h, j: (i, j, h)),
        out_shape=jax.ShapeDtypeStruct((b, s, d), BF16),
        scratch_shapes=[
            pltpu.VMEM((hd, 2 * tq), BF16),
            pltpu.VMEM((1, 2 * tq), F32),
            pltpu.VMEM((1, 2 * tq), F32),
            pltpu.VMEM((hd, 2 * tq), F32),
            pltpu.VMEM((tk, 2 * tq), F32),
            pltpu.VMEM((tk, 2 * tq), F32),
            pltpu.VMEM((1, 2 * tq), F32),
            pltpu.VMEM((1, 2 * tq), F32),
        ],
        compiler_params=_params(("parallel", "parallel", "arbitrary")),
        name="da_attn",
    )(lv, subln, qT, k, vT, bias)


def _hg_in_body(x_ref, g_ref, w_ref, lbs_ref, q_ref, k_ref, v_ref, gate_ref, lf_ref, *, d, layer):
    lbs = lbs_ref[...]
    e = jnp.exp(lbs - jnp.max(lbs, axis=0, keepdims=True))
    sm = e / jnp.sum(e, axis=0, keepdims=True)
    lb = jnp.sum(sm[0:layer + 1], axis=0, keepdims=True) - sm[0:1]
    a = jnp.log(lb)
    log_1m_lb = jnp.log(1.0 - lb)
    tm = x_ref.shape[0]
    for i in range(tm // HG_IN_ROWS):
        rows = slice(i * HG_IN_ROWS, (i + 1) * HG_IN_ROWS)
        h = _rms(x_ref[rows, :], g_ref[...]).astype(BF16)
        q = _dot(h, w_ref[:, 0:d])
        q_ref[rows, :] = _silu(q).astype(BF16)
        f = _dot(h, w_ref[:, d:2 * d])
        log_sig = -(jnp.maximum(-f, 0.0) + jnp.log(1.0 + jnp.exp(-jnp.abs(f))))
        c = log_1m_lb + log_sig
        log_f = jnp.maximum(a, c) + jnp.log(1.0 + jnp.exp(-jnp.abs(a - c)))
        lf_ref[rows, :] = log_f
        k_ref[rows, :] = (1.0 - jnp.exp(log_f)).astype(BF16)
        v_ref[rows, :] = _dot(h, w_ref[:, 2 * d:3 * d]).astype(BF16)
        gate_ref[rows, :] = _dot(h, w_ref[:, 3 * d:4 * d]).astype(BF16)


def _hg_in(x, g, w, lbs, layer):
    b, s, d = x.shape
    tm = TOK_TILE
    tok = pl.BlockSpec((None, tm, d), lambda i, j: (i, j, 0))
    return pl.pallas_call(
        functools.partial(_hg_in_body, d=d, layer=layer),
        grid=(b, s // tm),
        in_specs=[tok, _resident((1, d)), _resident((d, 4 * d)), _resident(lbs.shape)],
        out_specs=[tok, tok, tok, tok, tok],
        out_shape=[jax.ShapeDtypeStruct((b, s, d), BF16)] * 4 + [jax.ShapeDtypeStruct((b, s, d), F32)],
        compiler_params=_params(("parallel", "parallel")),
        name="hg_in",
    )(x, g, w, lbs)


def _hg_core_body(q_ref, k_ref, v_ref, gate_ref, lf_ref, on_ref, o_ref,
                  st_s, g_s, q_s, k_s, v_s, o_s, a_s, *, tile, nh):
    c64, c16, hd = HG_CHUNK, HG_SUB, HG_DIM
    c8 = c16 // 2
    nsub = c64 // c16
    heads = [slice(h * hd, (h + 1) * hd) for h in range(nh)]

    @pl.when(pl.program_id(1) == 0)
    def _():
        st_s[...] = jnp.zeros(st_s.shape, F32)

    ri = lax.broadcasted_iota(jnp.int32, (c64, c64), 0)
    ci = lax.broadcasted_iota(jnp.int32, (c64, c64), 1)
    tri = jnp.where(ri >= ci, 1.0, 0.0).astype(BF16)
    tt = lax.broadcasted_iota(jnp.int32, (c8, 1), 0)
    on = on_ref[...]

    def chunk(c, carry, fast):
        base = pl.multiple_of(c * c64, c64)
        rows = pl.ds(base, c64)
        lf = lf_ref[rows, :]
        hi = lf.astype(BF16)
        r1 = lf - hi.astype(F32)
        mid = r1.astype(BF16)
        lo = (r1 - mid.astype(F32)).astype(BF16)
        gc = _dot(tri, hi) + _dot(tri, mid) + _dot(tri, lo)
        q = q_ref[rows, :].astype(F32)
        k = k_ref[rows, :].astype(F32)
        vb = v_ref[rows, :]
        g_s[...] = gc
        q_s[...] = q
        k_s[...] = k
        if not fast:
            v_s[...] = vb.astype(F32)
        g_last = gc[c64 - 1:c64, :]

        qg = (q * jnp.exp(gc)).astype(BF16)
        kd = (k * jnp.exp(g_last - gc)).astype(BF16)
        dec = jnp.exp(g_last)

        a_s[...] = jnp.zeros(a_s.shape, BF16)
        for i in range(0 if fast else 1, nsub):
            lo_r, hi_r = i * c16, (i + 1) * c16
            nk = hi_r if fast else lo_r
            if i == 0:
                qi = q_s[lo_r:hi_r, :] * jnp.exp(g_s[lo_r:hi_r, :])
                kj = k_s[0:nk, :] * jnp.exp(-g_s[0:nk, :])
            else:
                ref = g_s[lo_r - 1:lo_r, :]
                qi = q_s[lo_r:hi_r, :] * jnp.exp(g_s[lo_r:hi_r, :] - ref)
                kj = k_s[0:nk, :] * jnp.exp(ref - g_s[0:nk, :])
            qi = qi.astype(BF16)
            kj = kj.astype(BF16)
            if fast:
                keep = (lax.broadcasted_iota(jnp.int32, (c16, nk), 1)
                        <= lax.broadcasted_iota(jnp.int32, (c16, nk), 0) + lo_r)
            for h, cols in enumerate(heads):
                a = _dot_nt(qi[:, cols], kj[:, cols])
                if fast:
                    a = jnp.where(keep, a, 0.0)
                a_s[h, lo_r:hi_r, 0:nk] = a.astype(BF16)

        for h, cols in enumerate(heads):
            st = st_s[h]
            o_s[:, cols] = _dot_nt(qg[:, cols], st.astype(BF16)) + _dot(a_s[h], vb[:, cols])
            st_s[h] = st * dec[:, cols] + _dot_tn(vb[:, cols], kd[:, cols])

        grp = max(nh // 2, 1)
        if not fast:
            for i in range(nsub):
                for g0 in range(0, nh, grp):
                    wide = slice(g0 * hd, (g0 + grp) * hd)
                    sub = [slice(h * hd, (h + 1) * hd) for h in range(grp)]
                    r0, r1_, r2 = i * c16, i * c16 + c8, (i + 1) * c16
                    g_lo, g_hi = g_s[r0:r1_, wide], g_s[r1_:r2, wide]
                    q_lo, q_hi = q_s[r0:r1_, wide], q_s[r1_:r2, wide]
                    acc_lo = [None] * grp
                    acc_hi = [None] * grp
                    for s in range(c16):
                        j = r0 + s
                        gs, ks, vs = g_s[j:j + 1, wide], k_s[j:j + 1, wide], v_s[j:j + 1, wide]
                        pr_hi = q_hi * ks * jnp.exp(g_hi - gs)
                        if s < c8:
                            pr_lo = q_lo * ks * jnp.exp(g_lo - gs)
                        for h, cols in enumerate(sub):
                            a = jnp.sum(pr_hi[:, cols], axis=-1, keepdims=True)
                            if s >= c8:
                                a = jnp.where(tt >= s - c8, a, 0.0)
                            t_hi = a * vs[:, cols]
                            acc_hi[h] = t_hi if acc_hi[h] is None else acc_hi[h] + t_hi
                            if s < c8:
                                a = jnp.sum(pr_lo[:, cols], axis=-1, keepdims=True)
                                a = jnp.where(tt >= s, a, 0.0)
                                t_lo = a * vs[:, cols]
                                acc_lo[h] = t_lo if acc_lo[h] is None else acc_lo[h] + t_lo
                    for h in range(grp):
                        cols = slice((g0 + h) * hd, (g0 + h + 1) * hd)
                        o_s[r0:r1_, cols] += acc_lo[h]
                        o_s[r1_:r2, cols] += acc_hi[h]

        gate = gate_ref[rows, :].astype(F32)
        sg = _silu(gate)
        for cols in heads:
            o = _rms(o_s[:, cols], on)
            o_ref[rows, cols] = (o * sg[:, cols]).astype(BF16)
        return carry

    fast = jnp.min(lf_ref[...]) >= -HG_MAX_EXPONENT / c16

    @pl.when(fast)
    def _():
        lax.fori_loop(0, tile // c64, functools.partial(chunk, fast=True), 0)

    @pl.when(jnp.logical_not(fast))
    def _():
        lax.fori_loop(0, tile // c64, functools.partial(chunk, fast=False), 0)


def _hg_core(q, k, v, gate, lf, onorm):
    b, s, d = q.shape
    hd = HG_DIM
    nh = d // hd
    tile = HG_TILE
    blk = pl.BlockSpec((None, tile, d), lambda i, j: (i, j, 0))
    return pl.pallas_call(
        functools.partial(_hg_core_body, tile=tile, nh=nh),
        grid=(b, s // tile),
        in_specs=[blk, blk, blk, blk, blk, _resident(onorm.shape)],
        out_specs=blk,
        out_shape=jax.ShapeDtypeStruct((b, s, d), BF16),
        scratch_shapes=[
            pltpu.VMEM((nh, hd, hd), F32),
            pltpu.VMEM((HG_CHUNK, d), F32),
            pltpu.VMEM((HG_CHUNK, d), F32),
            pltpu.VMEM((HG_CHUNK, d), F32),
            pltpu.VMEM((HG_CHUNK, d), F32),
            pltpu.VMEM((HG_CHUNK, d), F32),
            pltpu.VMEM((nh, HG_CHUNK, HG_CHUNK), BF16),
        ],
        compiler_params=_params(("parallel", "arbitrary")),
        name="hg_core",
    )(q, k, v, gate, lf, onorm)


def _gelu(x):
    return 0.5 * x * (1.0 + lax.erf(x * (1.0 / math.sqrt(2.0))))


def _sgu_body(x_ref, g_ref, win_ref, vn_ref, ws_ref, bsT_ref, wout_ref, o_ref, a_s, *, d, tm):
    x = x_ref[...]
    h = _rms(x, g_ref[...]).astype(BF16)
    v = _gelu(_dot(h, win_ref[:, d:2 * d]))
    v = _rms(v, vn_ref[...]).astype(BF16)
    gd = d // SG_GROUPS
    ri = lax.broadcasted_iota(jnp.int32, (SG_CHUNK, SG_CHUNK), 0)
    ci = lax.broadcasted_iota(jnp.int32, (SG_CHUNK, SG_CHUNK), 1)
    bsT = bsT_ref[...]
    for pair in range(SG_GROUPS // 2):
        u2 = _gelu(_dot(h, win_ref[:, 2 * pair * gd:2 * (pair + 1) * gd]))
        for half in range(2):
            gidx = 2 * pair + half
            cols = slice(gidx * gd, (gidx + 1) * gd)
            w = jnp.where(ri >= ci, ws_ref[gidx], 0.0).astype(BF16)
            bias = bsT[:, gidx:gidx + 1]
            for c in range(tm // SG_CHUNK):
                rows = slice(c * SG_CHUNK, (c + 1) * SG_CHUNK)
                mixed = _dot(w, v[rows, cols]) + bias
                a_s[rows, cols] = (u2[rows, half * gd:(half + 1) * gd] * mixed).astype(BF16)
    o_ref[...] = x + _dot(a_s[...], wout_ref[...])


def _sgu(x, g, w_in, vnorm, w_s, b_sT, w_out):
    b, s, d = x.shape
    tm = TOK_TILE
    tok = pl.BlockSpec((None, tm, d), lambda i, j: (i, j, 0))
    return pl.pallas_call(
        functools.partial(_sgu_body, d=d, tm=tm),
        grid=(b, s // tm),
        in_specs=[tok, _resident((1, d)), _resident(w_in.shape), _resident((1, d)),
                  _resident(w_s.shape), _resident(b_sT.shape), _resident(w_out.shape)],
        out_specs=tok,
        out_shape=jax.ShapeDtypeStruct((b, s, d), F32),
        scratch_shapes=[pltpu.VMEM((tm, d), BF16)],
        compiler_params=_params(("parallel", "parallel")),
        name="sgu",
    )(x, g, w_in, vnorm, w_s, b_sT, w_out)


def _kv_mem_body(mem_ref, g_ref, w_ref, k_ref, v_ref, *, d):
    h = _rms(mem_ref[...], g_ref[...]).astype(BF16)
    k_ref[...] = _dot(h, w_ref[:, 0:d]).astype(BF16)
    v_ref[...] = _dot(h, w_ref[:, d:2 * d]).astype(BF16)


def _kv_mem(mem, g, w_kv):
    b, m, d = mem.shape
    depth = w_kv.shape[0]
    return pl.pallas_call(
        functools.partial(_kv_mem_body, d=d),
        grid=(depth, b),
        in_specs=[
            pl.BlockSpec((None, m, d), lambda l, i: (i, 0, 0)),
            _resident((1, d)),
            pl.BlockSpec((None, d, 2 * d), lambda l, i: (l, 0, 0)),
        ],
        out_specs=[
            pl.BlockSpec((None, None, m, d), lambda l, i: (l, i, 0, 0)),
            pl.BlockSpec((None, None, m, d), lambda l, i: (l, i, 0, 0)),
        ],
        out_shape=[
            jax.ShapeDtypeStruct((depth, b, m, d), BF16),
            jax.ShapeDtypeStruct((depth, b, m, d), BF16),
        ],
        compiler_params=_params(("parallel", "parallel")),
        name="kv_mem",
    )(mem, g, w_kv)


def _cross_body(*refs, d, q_scale, fused_proj):
    if fused_proj:
        x_ref, ap_ref, wp_ref, g_ref, wq_ref, k_ref, v_ref, wo_ref, o_ref, a_s, q_s, x_s = refs
    else:
        x_ref, g_ref, wq_ref, k_ref, v_ref, wo_ref, o_ref, a_s, q_s, x_s = refs
    hd = d // CA_HEADS
    tm = x_ref.shape[0]
    parts = [slice(i * CA_ROWS, (i + 1) * CA_ROWS) for i in range(tm // CA_ROWS)]
    for rows in parts:
        x = x_ref[rows, :]
        if fused_proj:
            x = x + _dot(ap_ref[rows, :], wp_ref[...])
        x_s[rows, :] = x
        h = _rms(x, g_ref[...]).astype(BF16)
        q_s[rows, :] = (_dot(h, wq_ref[...]) * q_scale).astype(BF16)
    for rows in parts:
        for hh in range(CA_HEADS):
            cols = slice(hh * hd, (hh + 1) * hd)
            s = _dot_nt(q_s[rows, cols], k_ref[:, cols])
            m = jnp.max(s, axis=-1, keepdims=True)
            p = jnp.exp2(s - m)
            l = jnp.sum(p, axis=-1, keepdims=True)
            o = _dot(p.astype(BF16), v_ref[:, cols]) * (1.0 / l)
            a_s[rows, cols] = o.astype(BF16)
    for rows in parts:
        o_ref[rows, :] = x_s[rows, :] + _dot(a_s[rows, :], wo_ref[...])


def _cross(x, g, wq, k, v, wo, layer, proj=None):
    b, s, d = x.shape
    m = v.shape[2]
    tm = TOK_TILE
    tok = pl.BlockSpec((None, tm, d), lambda i, j: (i, j, 0))
    q_scale = (d // CA_HEADS) ** -0.5 * LOG2E
    pre_specs, pre_args = ([tok, _resident(proj[1].shape)], list(proj)) if proj is not None else ([], [])
    return pl.pallas_call(
        functools.partial(_cross_body, d=d, q_scale=q_scale, fused_proj=proj is not None),
        grid=(b, s // tm),
        in_specs=[tok] + pre_specs + [
            _resident((1, d)), _resident(wq.shape),
            pl.BlockSpec((None, None, m, d), lambda i, j: (layer, i, 0, 0)),
            pl.BlockSpec((None, None, m, d), lambda i, j: (layer, i, 0, 0)),
            _resident(wo.shape),
        ],
        out_specs=tok,
        out_shape=jax.ShapeDtypeStruct((b, s, d), F32),
        scratch_shapes=[pltpu.VMEM((tm, d), BF16), pltpu.VMEM((tm, d), BF16), pltpu.VMEM((tm, d), F32)],
        compiler_params=_params(("parallel", "parallel")),
        name="cross",
    )(x, *pre_args, g, wq, k, v, wo)


def _ffn_body(x_ref, g_ref, wgu_ref, wd_ref, gf_ref, o_ref, a_s, *, ff, final_norm):
    x = x_ref[...]
    h = _rms(x, g_ref[...]).astype(BF16)
    fc = FFN_CHUNK
    for c in range(ff // fc):
        gate = _dot(h, wgu_ref[:, c * fc:(c + 1) * fc])
        up = _dot(h, wgu_ref[:, ff + c * fc:ff + (c + 1) * fc])
        a_s[:, c * fc:(c + 1) * fc] = (_silu(gate) * up).astype(BF16)
    y = x + _dot(a_s[...], wd_ref[...])
    if final_norm:
        y = _rms(y, gf_ref[...])
    o_ref[...] = y


def _ffn(x, g, wgu, wd, gf, final_norm):
    b, s, d = x.shape
    ff = wd.shape[0]
    tm = TOK_TILE
    tok = pl.BlockSpec((None, tm, d), lambda i, j: (i, j, 0))
    return pl.pallas_call(
        functools.partial(_ffn_body, ff=ff, final_norm=final_norm),
        grid=(b, s // tm),
        in_specs=[tok, _resident((1, d)), _resident(wgu.shape), _resident(wd.shape), _resident((1, d))],
        out_specs=tok,
        out_shape=jax.ShapeDtypeStruct((b, s, d), F32),
        scratch_shapes=[pltpu.VMEM((tm, ff), BF16)],
        compiler_params=_params(("parallel", "parallel")),
        name="ffn",
    )(x, g, wgu, wd, gf)


def kernel(x, mem, rel_bias, norm_mix, norm_cross, norm_ffn, norm_mem, norm_final,
           da_w_in, da_w_out, da_lq1, da_lk1, da_lq2, da_lk2, da_subln,
           hg_w_in, hg_w_out, hg_lower_bounds, hg_onorm,
           sg_w_in, sg_w_out, sg_vnorm, sg_w_s, sg_b_s,
           ca_w_q, ca_w_kv, ca_w_o, ffn_w_gu, ffn_w_down):
    depth = norm_mix.shape[0]
    row = lambda vec: vec.reshape(1, -1)
    bf = lambda w: w.astype(BF16)

    k_mem, v_mem = _kv_mem(mem, row(norm_mem), bf(ca_w_kv))
    bias = _da_bias(rel_bias, ATT_TILE)
    da_scale = DA_QK_DIM ** -0.5 * LOG2E

    for i in range(depth):
        kind, j = i % N_MIXERS, i // N_MIXERS
        g_mix = row(norm_mix[i])
        if kind == 0:
            lam_init = 0.8 - 0.6 * math.exp(-0.3 * i)
            qT, k, vT = _da_in(x, g_mix, bf(da_w_in[j]), da_scale)
            lv = jnp.stack([da_lq1[j], da_lk1[j], da_lq2[j], da_lk2[j]])
            a = _da_attn(lv, row(da_subln[j]), qT, k, vT, bias, lam_init)
            proj = (a, bf(da_w_out[j]))
        elif kind == 1:
            q, k, v, gate, lf = _hg_in(x, g_mix, bf(hg_w_in[j]), hg_lower_bounds, i)
            a = _hg_core(q, k, v, gate, lf, row(hg_onorm[j]))
            proj = (a, bf(hg_w_out[j]))
        else:
            x = _sgu(x, g_mix, bf(sg_w_in[j]), row(sg_vnorm[j]), sg_w_s[j], sg_b_s[j].T, bf(sg_w_out[j]))
            proj = None
        x = _cross(x, row(norm_cross[i]), bf(ca_w_q[i]), k_mem, v_mem, bf(ca_w_o[i]), i, proj)
        x = _ffn(x, row(norm_ffn[i]), bf(ffn_w_gu[i]), bf(ffn_w_down[i]), row(norm_final), i == depth - 1)
    return x
```

```python
import functools
import math

import numpy as np
import jax
import jax.numpy as jnp
from jax import lax
from jax.experimental import pallas as pl
from jax.experimental.pallas import tpu as pltpu

F32 = jnp.float32
BF16 = jnp.bfloat16

EPS = 1e-6
LOG2E = math.log2(math.e)
NEG_BIG = -1e30

DA_HEADS = 8
DA_QK_DIM = 64
DA_V_DIM = 128
REL_BUCKETS = 32
REL_MAX_DIST = 128
HG_HEADS = 8
HG_DIM = 128
HG_CHUNK = 64
HG_SUB = 16
HG_MAX_EXPONENT = 80.0
SG_CHUNK = 128
SG_GROUPS = 8
CA_HEADS = 4
N_MIXERS = 3

VMEM_LIMIT = 56 * 1024 * 1024
TOK_TILE = 512
ATT_TILE = 512
ATT_STRIP = 256
HG_TILE = 512
FFN_CHUNK = 256
CA_ROWS = 256
HG_IN_ROWS = 256


def _params(sem):
    return pltpu.CompilerParams(dimension_semantics=sem, vmem_limit_bytes=VMEM_LIMIT)


def _resident(shape):
    nd = len(shape)
    return pl.BlockSpec(shape, lambda *_: (0,) * nd, pipeline_mode=pl.Buffered(1))


def _layer(w, layer):
    nd = w.ndim - 1
    return pl.BlockSpec((None,) + w.shape[1:], lambda *_: (layer,) + (0,) * nd, pipeline_mode=pl.Buffered(1))


def _rms(x, g):
    ms = jnp.mean(x * x, axis=-1, keepdims=True)
    return x * lax.rsqrt(ms + EPS) * g


def _dot(a, b):
    return jnp.dot(a, b, preferred_element_type=F32)


def _dot_nt(a, b):
    return lax.dot_general(a, b, (((1,), (1,)), ((), ())), preferred_element_type=F32)


def _dot_tn(a, b):
    return lax.dot_general(a, b, (((0,), (0,)), ((), ())), preferred_element_type=F32)


def _silu(x):
    return x * (1.0 / (1.0 + jnp.exp(-x)))


def _da_in_body(x_ref, g_ref, w_ref, qT_ref, k_ref, vT_ref, t_s, *, d, q_scale):
    h = _rms(x_ref[...], g_ref[...]).astype(BF16)
    t_s[...] = _dot(h, w_ref[:, 0:d]) * q_scale
    qT_ref[...] = t_s[...].T.astype(BF16)
    k_ref[...] = _dot(h, w_ref[:, d:2 * d]).astype(BF16)
    t_s[...] = _dot(h, w_ref[:, 2 * d:3 * d])
    vT_ref[...] = t_s[...].T.astype(BF16)


def _da_in(x, g, w, j, q_scale):
    b, s, d = x.shape
    tm = TOK_TILE
    return pl.pallas_call(
        functools.partial(_da_in_body, d=d, q_scale=q_scale),
        grid=(b, s // tm),
        in_specs=[
            pl.BlockSpec((None, tm, d), lambda i, j: (i, j, 0)),
            _resident((1, d)),
            _layer(w, j),
        ],
        out_specs=[
            pl.BlockSpec((None, d, tm), lambda i, j: (i, 0, j)),
            pl.BlockSpec((None, tm, d), lambda i, j: (i, j, 0)),
            pl.BlockSpec((None, d, tm), lambda i, j: (i, 0, j)),
        ],
        out_shape=[
            jax.ShapeDtypeStruct((b, d, s), BF16),
            jax.ShapeDtypeStruct((b, s, d), BF16),
            jax.ShapeDtypeStruct((b, d, s), BF16),
        ],
        scratch_shapes=[pltpu.VMEM((tm, d), F32)],
        compiler_params=_params(("parallel", "parallel")),
        name="da_in",
    )(x, g, w)


def _rel_bucket_np(dist):
    n = np.maximum(dist, 0)
    exact = REL_BUCKETS // 2
    nf = np.maximum(n, exact).astype(np.float32)
    large = exact + (np.log(nf / np.float32(exact)) / np.float32(math.log(REL_MAX_DIST / exact))
                     * np.float32(REL_BUCKETS - exact)).astype(np.int32)
    large = np.minimum(large, REL_BUCKETS - 1)
    return np.where(n < exact, n, large).astype(np.int32)


def _bucket_tiles(t):
    kk = np.arange(t, dtype=np.int64)[:, None]
    qq = np.arange(t, dtype=np.int64)[None, :]
    d_diag = qq - kk
    diag = np.where(d_diag >= 0, _rel_bucket_np(d_diag), -1).astype(np.int32)
    off = _rel_bucket_np(qq + t - kk)
    assert int(_rel_bucket_np(np.arange(t + 1, 8 * t)).min()) == REL_BUCKETS - 1
    return diag, off


def _da_bias_body(rb_ref, bd_ref, bo_ref, out_ref, *, t):
    h = pl.program_id(0)
    far = rb_ref[h, REL_BUCKETS - 1]
    for which, bref in ((0, bd_ref), (1, bo_ref)):
        bk = bref[...]
        acc = jnp.zeros((t, t), F32)
        for bb in range(REL_BUCKETS - 1):
            acc = jnp.where(bk == bb, rb_ref[h, bb] - far, acc)
        out_ref[which] = jnp.where(bk < 0, NEG_BIG, acc * LOG2E)


def _da_bias(rel_bias, t):
    diag, off = _bucket_tiles(t)
    hh = rel_bias.shape[0]
    return pl.pallas_call(
        functools.partial(_da_bias_body, t=t),
        grid=(hh,),
        in_specs=[
            pl.BlockSpec(memory_space=pltpu.SMEM),
            _resident((t, t)),
            _resident((t, t)),
        ],
        out_specs=pl.BlockSpec((None, 2, t, t), lambda i: (i, 0, 0, 0)),
        out_shape=jax.ShapeDtypeStruct((hh, 2, t, t), F32),
        compiler_params=_params(("parallel",)),
        name="da_bias",
    )(rel_bias, jnp.asarray(diag), jnp.asarray(off))


def _da_attn_body(lv_ref, sub_ref, qT_ref, k_ref, vT_ref, bias_ref, o_ref,
                  qb_s, m_s, l_s, acc_s, s0_s, s1_s, mx0_s, mx1_s, *, tq, tk, lam_init):
    qi = pl.program_id(2)
    sw = ATT_STRIP
    per_half = tk // sw
    qT = qT_ref[...]
    row = lax.broadcasted_iota(jnp.int32, qT.shape, 0)
    zero = jnp.zeros_like(qT)
    qb_s[:, 0:tq] = jnp.where(row < DA_QK_DIM, qT, zero)
    qb_s[:, tq:2 * tq] = jnp.where(row >= DA_QK_DIM, qT, zero)
    m_s[...] = jnp.full(m_s.shape, NEG_BIG, F32)
    l_s[...] = jnp.zeros(l_s.shape, F32)
    acc_s[...] = jnp.zeros(acc_s.shape, F32)

    kinds = {"far": (None, None), "pre": (1, None), "d0": (0, 1), "d1": ("skip", 0)}

    def strips(kind):
        for comp in range(2):
            for r in range(tq // sw):
                which = kinds[kind][r // per_half]
                if which == "skip":
                    continue
                yield slice(comp * tq + r * sw, comp * tq + (r + 1) * sw), which, (r % per_half) * sw

    def a_strip(kt, bufs, cols, which, qoff):
        buf, mx = bufs
        s = _dot(kt, qb_s[:, cols])
        if which is not None:
            s = s + bias_ref[which, :, qoff:qoff + sw]
        buf[:, cols] = s
        mx[:, cols] = jnp.max(s, axis=0, keepdims=True)

    def b_strip(vt, bufs, cols):
        buf, mx = bufs
        m_old = m_s[:, cols]
        m_new = jnp.maximum(m_old, mx[:, cols])
        alpha = jnp.exp2(m_old - m_new)
        p = jnp.exp2(buf[:, cols] - m_new)
        l_s[:, cols] = alpha * l_s[:, cols] + jnp.sum(p, axis=0, keepdims=True)
        acc_s[:, cols] = alpha * acc_s[:, cols] + _dot(vt, p.astype(BF16))
        m_s[:, cols] = m_new

    def step(a=None, b=None):
        sa, sb = [], []
        if a is not None:
            kj, kind, bufs_a = a
            kt = k_ref[pl.ds(pl.multiple_of(kj * tk, tk), tk), :]
            sa = list(strips(kind))
        if b is not None:
            kj, kind, bufs_b = b
            vt = vT_ref[:, pl.ds(pl.multiple_of(kj * tk, tk), tk)]
            sb = list(strips(kind))
        for n in range(max(len(sa), len(sb))):
            if n < len(sa):
                a_strip(kt, bufs_a, *sa[n])
            if n < len(sb):
                b_strip(vt, bufs_b, sb[n][0])

    b0, b1 = (s0_s, mx0_s), (s1_s, mx1_s)

    @pl.when(qi == 0)
    def _():
        step(a=(0, "d0", b0))
        step(a=(1, "d1", b1), b=(0, "d0", b0))
        step(b=(1, "d1", b1))

    @pl.when(qi > 0)
    def _():
        step(a=(0, "far", b0))

        def far_steps(base, count):
            for t in range(count):
                even = t % 2 == 0
                step(a=(base + t + 1, "far", b1 if even else b0), b=(base + t, "far", b0 if even else b1))

        def far4(i, carry):
            far_steps(4 * i, 4)
            return carry

        n_pairs = qi - 1
        lax.fori_loop(0, n_pairs // 2, far4, 0)

        @pl.when(n_pairs % 2 == 1)
        def _():
            far_steps(2 * (n_pairs - 1), 2)

        step(a=(2 * qi - 1, "pre", b1), b=(2 * qi - 2, "far", b0))
        step(a=(2 * qi, "d0", b0), b=(2 * qi - 1, "pre", b1))
        step(a=(2 * qi + 1, "d1", b1), b=(2 * qi, "d0", b0))
        step(b=(2 * qi + 1, "d1", b1))

    lv = lv_ref[...]
    s1 = jnp.sum(lv[0:1] * lv[1:2], axis=-1, keepdims=True)
    s2 = jnp.sum(lv[2:3] * lv[3:4], axis=-1, keepdims=True)
    lam = jnp.exp(s1) - jnp.exp(s2) + lam_init
    o2 = acc_s[...] * (1.0 / l_s[...])
    oT = o2[:, 0:tq] - lam * o2[:, tq:2 * tq]
    o = oT.T
    o = _rms(o, sub_ref[...]) * (1.0 - lam_init)
    o_ref[...] = o.astype(BF16)


def _da_attn(lv, subln, qT, k, vT, bias, lam_init):
    b, d, s = qT.shape
    tq, tk = 2 * ATT_TILE, ATT_TILE
    hd = DA_V_DIM
    nh = d // hd
    return pl.pallas_call(
        functools.partial(_da_attn_body, tq=tq, tk=tk, lam_init=lam_init),
        grid=(b, nh, s // tq),
        in_specs=[
            _resident(lv.shape),
            _resident(subln.shape),
            pl.BlockSpec((None, hd, tq), lambda i, h, j: (i, h, j)),
            pl.BlockSpec((None, s, hd), lambda i, h, j: (i, 0, h)),
            pl.BlockSpec((None, hd, s), lambda i, h, j: (i, h, 0)),
            pl.BlockSpec((None, 2, tk, tk), lambda i, h, j: (h, 0, 0, 0)),
        ],
        out_specs=pl.BlockSpec((None, tq, hd), lambda i, h, j: (i, j, h)),
        out_shape=jax.ShapeDtypeStruct((b, s, d), BF16),
        scratch_shapes=[
            pltpu.VMEM((hd, 2 * tq), BF16),
            pltpu.VMEM((1, 2 * tq), F32),
            pltpu.VMEM((1, 2 * tq), F32),
            pltpu.VMEM((hd, 2 * tq), F32),
            pltpu.VMEM((tk, 2 * tq), F32),
            pltpu.VMEM((tk, 2 * tq), F32),
            pltpu.VMEM((1, 2 * tq), F32),
            pltpu.VMEM((1, 2 * tq), F32),
        ],
        compiler_params=_params(("parallel", "parallel", "arbitrary")),
        name="da_attn",
    )(lv, subln, qT, k, vT, bias)


def _hg_in_body(x_ref, g_ref, w_ref, lbs_ref, q_ref, k_ref, v_ref, gate_ref, lf_ref, *, d, layer):
    lbs = lbs_ref[...]
    e = jnp.exp(lbs - jnp.max(lbs, axis=0, keepdims=True))
    sm = e / jnp.sum(e, axis=0, keepdims=True)
    lb = jnp.sum(sm[0:layer + 1], axis=0, keepdims=True) - sm[0:1]
    a = jnp.log(lb)
    log_1m_lb = jnp.log(1.0 - lb)
    tm = x_ref.shape[0]
    for i in range(tm // HG_IN_ROWS):
        rows = slice(i * HG_IN_ROWS, (i + 1) * HG_IN_ROWS)
        h = _rms(x_ref[rows, :], g_ref[...]).astype(BF16)
        q = _dot(h, w_ref[:, 0:d])
        q_ref[rows, :] = _silu(q).astype(BF16)
        f = _dot(h, w_ref[:, d:2 * d])
        log_sig = -(jnp.maximum(-f, 0.0) + jnp.log(1.0 + jnp.exp(-jnp.abs(f))))
        c = log_1m_lb + log_sig
        log_f = jnp.maximum(a, c) + jnp.log(1.0 + jnp.exp(-jnp.abs(a - c)))
        lf_ref[rows, :] = log_f
        k_ref[rows, :] = (1.0 - jnp.exp(log_f)).astype(BF16)
        v_ref[rows, :] = _dot(h, w_ref[:, 2 * d:3 * d]).astype(BF16)
        gate_ref[rows, :] = _dot(h, w_ref[:, 3 * d:4 * d]).astype(BF16)


def _hg_in(x, g, w, j, lbs, layer):
    b, s, d = x.shape
    tm = TOK_TILE
    tok = pl.BlockSpec((None, tm, d), lambda i, j: (i, j, 0))
    return pl.pallas_call(
        functools.partial(_hg_in_body, d=d, layer=layer),
        grid=(b, s // tm),
        in_specs=[tok, _resident((1, d)), _layer(w, j), _resident(lbs.shape)],
        out_specs=[tok, tok, tok, tok, tok],
        out_shape=[jax.ShapeDtypeStruct((b, s, d), BF16)] * 4 + [jax.ShapeDtypeStruct((b, s, d), F32)],
        compiler_params=_params(("parallel", "parallel")),
        name="hg_in",
    )(x, g, w, lbs)


def _hg_core_body(q_ref, k_ref, v_ref, gate_ref, lf_ref, on_ref, o_ref,
                  st_s, g_s, q_s, k_s, v_s, o_s, a_s, *, tile, nh):
    c64, c16, hd = HG_CHUNK, HG_SUB, HG_DIM
    c8 = c16 // 2
    nsub = c64 // c16
    heads = [slice(h * hd, (h + 1) * hd) for h in range(nh)]

    @pl.when(pl.program_id(1) == 0)
    def _():
        st_s[...] = jnp.zeros(st_s.shape, F32)

    ri = lax.broadcasted_iota(jnp.int32, (c64, c64), 0)
    ci = lax.broadcasted_iota(jnp.int32, (c64, c64), 1)
    tri = jnp.where(ri >= ci, 1.0, 0.0).astype(BF16)
    tt = lax.broadcasted_iota(jnp.int32, (c8, 1), 0)
    on = on_ref[...]

    def chunk(c, carry, fast):
        base = pl.multiple_of(c * c64, c64)
        rows = pl.ds(base, c64)
        lf = lf_ref[rows, :]
        hi = lf.astype(BF16)
        r1 = lf - hi.astype(F32)
        mid = r1.astype(BF16)
        lo = (r1 - mid.astype(F32)).astype(BF16)
        gc = _dot(tri, hi) + _dot(tri, mid) + _dot(tri, lo)
        q = q_ref[rows, :].astype(F32)
        k = k_ref[rows, :].astype(F32)
        vb = v_ref[rows, :]
        g_s[...] = gc
        q_s[...] = q
        k_s[...] = k
        if not fast:
            v_s[...] = vb.astype(F32)
        g_last = gc[c64 - 1:c64, :]

        qg = (q * jnp.exp(gc)).astype(BF16)
        kd = (k * jnp.exp(g_last - gc)).astype(BF16)
        dec = jnp.exp(g_last)

        a_s[...] = jnp.zeros(a_s.shape, BF16)
        for i in range(0 if fast else 1, nsub):
            lo_r, hi_r = i * c16, (i + 1) * c16
            nk = hi_r if fast else lo_r
            if i == 0:
                qi = q_s[lo_r:hi_r, :] * jnp.exp(g_s[lo_r:hi_r, :])
                kj = k_s[0:nk, :] * jnp.exp(-g_s[0:nk, :])
            else:
                ref = g_s[lo_r - 1:lo_r, :]
                qi = q_s[lo_r:hi_r, :] * jnp.exp(g_s[lo_r:hi_r, :] - ref)
                kj = k_s[0:nk, :] * jnp.exp(ref - g_s[0:nk, :])
            qi = qi.astype(BF16)
            kj = kj.astype(BF16)
            if fast:
                keep = (lax.broadcasted_iota(jnp.int32, (c16, nk), 1)
                        <= lax.broadcasted_iota(jnp.int32, (c16, nk), 0) + lo_r)
            for h, cols in enumerate(heads):
                a = _dot_nt(qi[:, cols], kj[:, cols])
                if fast:
                    a = jnp.where(keep, a, 0.0)
                a_s[h, lo_r:hi_r, 0:nk] = a.astype(BF16)

        for h, cols in enumerate(heads):
            st = st_s[h]
            o_s[:, cols] = _dot_nt(qg[:, cols], st.astype(BF16)) + _dot(a_s[h], vb[:, cols])
            st_s[h] = st * dec[:, cols] + _dot_tn(vb[:, cols], kd[:, cols])

        grp = max(nh // 2, 1)
        if not fast:
            for i in range(nsub):
                for g0 in range(0, nh, grp):
                    wide = slice(g0 * hd, (g0 + grp) * hd)
                    sub = [slice(h * hd, (h + 1) * hd) for h in range(grp)]
                    r0, r1_, r2 = i * c16, i * c16 + c8, (i + 1) * c16
                    g_lo, g_hi = g_s[r0:r1_, wide], g_s[r1_:r2, wide]
                    q_lo, q_hi = q_s[r0:r1_, wide], q_s[r1_:r2, wide]
                    acc_lo = [None] * grp
                    acc_hi = [None] * grp
                    for s in range(c16):
                        j = r0 + s
                        gs, ks, vs = g_s[j:j + 1, wide], k_s[j:j + 1, wide], v_s[j:j + 1, wide]
                        pr_hi = q_hi * ks * jnp.exp(g_hi - gs)
                        if s < c8:
                            pr_lo = q_lo * ks * jnp.exp(g_lo - gs)
                        for h, cols in enumerate(sub):
                            a = jnp.sum(pr_hi[:, cols], axis=-1, keepdims=True)
                            if s >= c8:
                                a = jnp.where(tt >= s - c8, a, 0.0)
                            t_hi = a * vs[:, cols]
                            acc_hi[h] = t_hi if acc_hi[h] is None else acc_hi[h] + t_hi
                            if s < c8:
                                a = jnp.sum(pr_lo[:, cols], axis=-1, keepdims=True)
                                a = jnp.where(tt >= s, a, 0.0)
                                t_lo = a * vs[:, cols]
                                acc_lo[h] = t_lo if acc_lo[h] is None else acc_lo[h] + t_lo
                    for h in range(grp):
                        cols = slice((g0 + h) * hd, (g0 + h + 1) * hd)
                        o_s[r0:r1_, cols] += acc_lo[h]
                        o_s[r1_:r2, cols] += acc_hi[h]

        gate = gate_ref[rows, :].astype(F32)
        sg = _silu(gate)
        for cols in heads:
            o = _rms(o_s[:, cols], on)
            o_ref[rows, cols] = (o * sg[:, cols]).astype(BF16)
        return carry

    fast = jnp.min(lf_ref[...]) >= -HG_MAX_EXPONENT / c16

    @pl.when(fast)
    def _():
        lax.fori_loop(0, tile // c64, functools.partial(chunk, fast=True), 0, unroll=4)

    @pl.when(jnp.logical_not(fast))
    def _():
        lax.fori_loop(0, tile // c64, functools.partial(chunk, fast=False), 0)


def _hg_core(q, k, v, gate, lf, onorm):
    b, s, d = q.shape
    hd = HG_DIM
    nh = d // hd
    tile = HG_TILE
    blk = pl.BlockSpec((None, tile, d), lambda i, j: (i, j, 0))
    return pl.pallas_call(
        functools.partial(_hg_core_body, tile=tile, nh=nh),
        grid=(b, s // tile),
        in_specs=[blk, blk, blk, blk, blk, _resident(onorm.shape)],
        out_specs=blk,
        out_shape=jax.ShapeDtypeStruct((b, s, d), BF16),
        scratch_shapes=[
            pltpu.VMEM((nh, hd, hd), F32),
            pltpu.VMEM((HG_CHUNK, d), F32),
            pltpu.VMEM((HG_CHUNK, d), F32),
            pltpu.VMEM((HG_CHUNK, d), F32),
            pltpu.VMEM((HG_CHUNK, d), F32),
            pltpu.VMEM((HG_CHUNK, d), F32),
            pltpu.VMEM((nh, HG_CHUNK, HG_CHUNK), BF16),
        ],
        compiler_params=_params(("parallel", "arbitrary")),
        name="hg_core",
    )(q, k, v, gate, lf, onorm)


def _gelu(x):
    return 0.5 * x * (1.0 + lax.erf(x * (1.0 / math.sqrt(2.0))))


def _sgu_body(x_ref, g_ref, win_ref, vn_ref, ws_ref, bsT_ref, wout_ref, o_ref, a_s, *, d, tm):
    x = x_ref[...]
    h = _rms(x, g_ref[...]).astype(BF16)
    v = _gelu(_dot(h, win_ref[:, d:2 * d]))
    v = _rms(v, vn_ref[...]).astype(BF16)
    gd = d // SG_GROUPS
    ri = lax.broadcasted_iota(jnp.int32, (SG_CHUNK, SG_CHUNK), 0)
    ci = lax.broadcasted_iota(jnp.int32, (SG_CHUNK, SG_CHUNK), 1)
    bsT = bsT_ref[...]
    for pair in range(SG_GROUPS // 2):
        u2 = _gelu(_dot(h, win_ref[:, 2 * pair * gd:2 * (pair + 1) * gd]))
        for half in range(2):
            gidx = 2 * pair + half
            cols = slice(gidx * gd, (gidx + 1) * gd)
            w = jnp.where(ri >= ci, ws_ref[gidx], 0.0).astype(BF16)
            bias = bsT[:, gidx:gidx + 1]
            for c in range(tm // SG_CHUNK):
                rows = slice(c * SG_CHUNK, (c + 1) * SG_CHUNK)
                mixed = _dot(w, v[rows, cols]) + bias
                a_s[rows, cols] = (u2[rows, half * gd:(half + 1) * gd] * mixed).astype(BF16)
    o_ref[...] = x + _dot(a_s[...], wout_ref[...])


def _sgu(x, g, w_in, vnorm, w_s, b_sT, w_out, j):
    b, s, d = x.shape
    tm = TOK_TILE
    tok = pl.BlockSpec((None, tm, d), lambda i, j: (i, j, 0))
    return pl.pallas_call(
        functools.partial(_sgu_body, d=d, tm=tm),
        grid=(b, s // tm),
        in_specs=[tok, _resident((1, d)), _layer(w_in, j), _resident((1, d)),
                  _resident(w_s.shape), _resident(b_sT.shape), _layer(w_out, j)],
        out_specs=tok,
        out_shape=jax.ShapeDtypeStruct((b, s, d), F32),
        scratch_shapes=[pltpu.VMEM((tm, d), BF16)],
        compiler_params=_params(("parallel", "parallel")),
        name="sgu",
    )(x, g, w_in, vnorm, w_s, b_sT, w_out)


def _kv_mem_body(mem_ref, g_ref, w_ref, k_ref, v_ref, *, d):
    h = _rms(mem_ref[...], g_ref[...]).astype(BF16)
    k_ref[...] = _dot(h, w_ref[:, 0:d]).astype(BF16)
    v_ref[...] = _dot(h, w_ref[:, d:2 * d]).astype(BF16)


def _kv_mem(mem, g, w_kv):
    b, m, d = mem.shape
    depth = w_kv.shape[0]
    return pl.pallas_call(
        functools.partial(_kv_mem_body, d=d),
        grid=(depth, b),
        in_specs=[
            pl.BlockSpec((None, m, d), lambda l, i: (i, 0, 0)),
            _resident((1, d)),
            pl.BlockSpec((None, d, 2 * d), lambda l, i: (l, 0, 0)),
        ],
        out_specs=[
            pl.BlockSpec((None, None, m, d), lambda l, i: (l, i, 0, 0)),
            pl.BlockSpec((None, None, m, d), lambda l, i: (l, i, 0, 0)),
        ],
        out_shape=[
            jax.ShapeDtypeStruct((depth, b, m, d), BF16),
            jax.ShapeDtypeStruct((depth, b, m, d), BF16),
        ],
        compiler_params=_params(("parallel", "parallel")),
        name="kv_mem",
    )(mem, g, w_kv)


def _cross_body(*refs, d, q_scale, fused_proj):
    if fused_proj:
        x_ref, ap_ref, wp_ref, g_ref, wq_ref, k_ref, v_ref, wo_ref, o_ref, a_s, q_s, x_s = refs
    else:
        x_ref, g_ref, wq_ref, k_ref, v_ref, wo_ref, o_ref, a_s, q_s, x_s = refs
    hd = d // CA_HEADS
    tm = x_ref.shape[0]
    parts = [slice(i * CA_ROWS, (i + 1) * CA_ROWS) for i in range(tm // CA_ROWS)]
    for rows in parts:
        x = x_ref[rows, :]
        if fused_proj:
            x = x + _dot(ap_ref[rows, :], wp_ref[...])
        x_s[rows, :] = x
        h = _rms(x, g_ref[...]).astype(BF16)
        q_s[rows, :] = (_dot(h, wq_ref[...]) * q_scale).astype(BF16)
    for rows in parts:
        for hh in range(CA_HEADS):
            cols = slice(hh * hd, (hh + 1) * hd)
            s = _dot_nt(q_s[rows, cols], k_ref[:, cols])
            m = jnp.max(s, axis=-1, keepdims=True)
            p = jnp.exp2(s - m)
            l = jnp.sum(p, axis=-1, keepdims=True)
            o = _dot(p.astype(BF16), v_ref[:, cols]) * (1.0 / l)
            a_s[rows, cols] = o.astype(BF16)
    for rows in parts:
        o_ref[rows, :] = x_s[rows, :] + _dot(a_s[rows, :], wo_ref[...])


def _cross(x, g, wq, k, v, wo, layer, proj=None):
    b, s, d = x.shape
    m = v.shape[2]
    tm = TOK_TILE
    tok = pl.BlockSpec((None, tm, d), lambda i, j: (i, j, 0))
    q_scale = (d // CA_HEADS) ** -0.5 * LOG2E
    pre_specs, pre_args = ([tok, _layer(proj[1], proj[2])], list(proj[:2])) if proj is not None else ([], [])
    return pl.pallas_call(
        functools.partial(_cross_body, d=d, q_scale=q_scale, fused_proj=proj is not None),
        grid=(b, s // tm),
        in_specs=[tok] + pre_specs + [
            _resident((1, d)), _layer(wq, layer),
            pl.BlockSpec((None, None, m, d), lambda i, j: (layer, i, 0, 0)),
            pl.BlockSpec((None, None, m, d), lambda i, j: (layer, i, 0, 0)),
            _layer(wo, layer),
        ],
        out_specs=tok,
        out_shape=jax.ShapeDtypeStruct((b, s, d), F32),
        scratch_shapes=[pltpu.VMEM((tm, d), BF16), pltpu.VMEM((tm, d), BF16), pltpu.VMEM((tm, d), F32)],
        compiler_params=_params(("parallel", "parallel")),
        name="cross",
    )(x, *pre_args, g, wq, k, v, wo)


def _ffn_body(x_ref, g_ref, wgu_ref, wd_ref, gf_ref, o_ref, a_s, *, ff, final_norm):
    x = x_ref[...]
    h = _rms(x, g_ref[...]).astype(BF16)
    fc = FFN_CHUNK
    for c in range(ff // fc):
        gate = _dot(h, wgu_ref[:, c * fc:(c + 1) * fc])
        up = _dot(h, wgu_ref[:, ff + c * fc:ff + (c + 1) * fc])
        a_s[:, c * fc:(c + 1) * fc] = (_silu(gate) * up).astype(BF16)
    y = x + _dot(a_s[...], wd_ref[...])
    if final_norm:
        y = _rms(y, gf_ref[...])
    o_ref[...] = y


def _ffn(x, g, wgu, wd, layer, gf, final_norm):
    b, s, d = x.shape
    ff = wd.shape[1]
    tm = TOK_TILE
    tok = pl.BlockSpec((None, tm, d), lambda i, j: (i, j, 0))
    return pl.pallas_call(
        functools.partial(_ffn_body, ff=ff, final_norm=final_norm),
        grid=(b, s // tm),
        in_specs=[tok, _resident((1, d)), _layer(wgu, layer), _layer(wd, layer), _resident((1, d))],
        out_specs=tok,
        out_shape=jax.ShapeDtypeStruct((b, s, d), F32),
        scratch_shapes=[pltpu.VMEM((tm, ff), BF16)],
        compiler_params=_params(("parallel", "parallel")),
        name="ffn",
    )(x, g, wgu, wd, gf)


def kernel(x, mem, rel_bias, norm_mix, norm_cross, norm_ffn, norm_mem, norm_final,
           da_w_in, da_w_out, da_lq1, da_lk1, da_lq2, da_lk2, da_subln,
           hg_w_in, hg_w_out, hg_lower_bounds, hg_onorm,
           sg_w_in, sg_w_out, sg_vnorm, sg_w_s, sg_b_s,
           ca_w_q, ca_w_kv, ca_w_o, ffn_w_gu, ffn_w_down):
    depth = norm_mix.shape[0]
    row = lambda vec: vec.reshape(1, -1)
    bf = lambda w: w.astype(BF16)

    k_mem, v_mem = _kv_mem(mem, row(norm_mem), bf(ca_w_kv))
    bias = _da_bias(rel_bias, ATT_TILE)
    da_scale = DA_QK_DIM ** -0.5 * LOG2E
    da_w_in, da_w_out, hg_w_in, hg_w_out = bf(da_w_in), bf(da_w_out), bf(hg_w_in), bf(hg_w_out)
    sg_w_in, sg_w_out = bf(sg_w_in), bf(sg_w_out)
    ca_w_q, ca_w_o, ffn_w_gu, ffn_w_down = bf(ca_w_q), bf(ca_w_o), bf(ffn_w_gu), bf(ffn_w_down)

    for i in range(depth):
        kind, j = i % N_MIXERS, i // N_MIXERS
        g_mix = row(norm_mix[i])
        if kind == 0:
            lam_init = 0.8 - 0.6 * math.exp(-0.3 * i)
            qT, k, vT = _da_in(x, g_mix, da_w_in, j, da_scale)
            lv = jnp.stack([da_lq1[j], da_lk1[j], da_lq2[j], da_lk2[j]])
            a = _da_attn(lv, row(da_subln[j]), qT, k, vT, bias, lam_init)
            proj = (a, da_w_out, j)
        elif kind == 1:
            q, k, v, gate, lf = _hg_in(x, g_mix, hg_w_in, j, hg_lower_bounds, i)
            a = _hg_core(q, k, v, gate, lf, row(hg_onorm[j]))
            proj = (a, hg_w_out, j)
        else:
            x = _sgu(x, g_mix, sg_w_in, row(sg_vnorm[j]), sg_w_s[j], sg_b_s[j].T, sg_w_out, j)
            proj = None
        x = _cross(x, row(norm_cross[i]), ca_w_q, k_mem, v_mem, ca_w_o, i, proj)
        x = _ffn(x, row(norm_ffn[i]), ffn_w_gu, ffn_w_down, i, row(norm_final), i == depth - 1)
    return x
```

```python
import functools
import math

import numpy as np
import jax
import jax.numpy as jnp
from jax import lax
from jax.experimental import pallas as pl
from jax.experimental.pallas import tpu as pltpu

F32 = jnp.float32
BF16 = jnp.bfloat16

EPS = 1e-6
LOG2E = math.log2(math.e)
NEG_BIG = -1e30

DA_HEADS = 8
DA_QK_DIM = 64
DA_V_DIM = 128
DA_SUM_ROWS = 16
REL_BUCKETS = 32
REL_MAX_DIST = 128
HG_HEADS = 8
HG_DIM = 128
HG_CHUNK = 64
HG_SUB = 16
HG_MAX_EXPONENT = 80.0
SG_CHUNK = 128
SG_GROUPS = 8
CA_HEADS = 4
N_MIXERS = 3

VMEM_LIMIT = 56 * 1024 * 1024
TOK_TILE = 512
ATT_TILE = 512
ATT_STRIP = 256
HG_TILE = 512
FFN_CHUNK = 256
CA_ROWS = 256
HG_IN_ROWS = 256


def _params(sem):
    return pltpu.CompilerParams(dimension_semantics=sem, vmem_limit_bytes=VMEM_LIMIT)


def _resident(shape):
    nd = len(shape)
    return pl.BlockSpec(shape, lambda *_: (0,) * nd, pipeline_mode=pl.Buffered(1))


def _layer(w, layer):
    nd = w.ndim - 1
    return pl.BlockSpec((None,) + w.shape[1:], lambda *_: (layer,) + (0,) * nd, pipeline_mode=pl.Buffered(1))


def _rms(x, g):
    ms = jnp.mean(x * x, axis=-1, keepdims=True)
    return x * lax.rsqrt(ms + EPS) * g


def _dot(a, b):
    return jnp.dot(a, b, preferred_element_type=F32)


def _dot_nt(a, b):
    return lax.dot_general(a, b, (((1,), (1,)), ((), ())), preferred_element_type=F32)


def _dot_tn(a, b):
    return lax.dot_general(a, b, (((0,), (0,)), ((), ())), preferred_element_type=F32)


def _silu(x):
    return x * (1.0 / (1.0 + jnp.exp(-x)))


def _da_in_body(x_ref, g_ref, w_ref, qT_ref, k_ref, vT_ref, t_s, *, d, q_scale):
    h = _rms(x_ref[...], g_ref[...]).astype(BF16)
    t_s[...] = _dot(h, w_ref[:, 0:d]) * q_scale
    qT_ref[...] = t_s[...].T.astype(BF16)
    k_ref[...] = _dot(h, w_ref[:, d:2 * d]).astype(BF16)
    t_s[...] = _dot(h, w_ref[:, 2 * d:3 * d])
    vT_ref[...] = t_s[...].T.astype(BF16)


def _da_in(x, g, w, j, q_scale):
    b, s, d = x.shape
    tm = TOK_TILE
    return pl.pallas_call(
        functools.partial(_da_in_body, d=d, q_scale=q_scale),
        grid=(b, s // tm),
        in_specs=[
            pl.BlockSpec((None, tm, d), lambda i, j: (i, j, 0)),
            _resident((1, d)),
            _layer(w, j),
        ],
        out_specs=[
            pl.BlockSpec((None, d, tm), lambda i, j: (i, 0, j)),
            pl.BlockSpec((None, tm, d), lambda i, j: (i, j, 0)),
            pl.BlockSpec((None, d, tm), lambda i, j: (i, 0, j)),
        ],
        out_shape=[
            jax.ShapeDtypeStruct((b, d, s), BF16),
            jax.ShapeDtypeStruct((b, s, d), BF16),
            jax.ShapeDtypeStruct((b, d, s), BF16),
        ],
        scratch_shapes=[pltpu.VMEM((tm, d), F32)],
        compiler_params=_params(("parallel", "parallel")),
        name="da_in",
    )(x, g, w)


def _rel_bucket_np(dist):
    n = np.maximum(dist, 0)
    exact = REL_BUCKETS // 2
    nf = np.maximum(n, exact).astype(np.float32)
    large = exact + (np.log(nf / np.float32(exact)) / np.float32(math.log(REL_MAX_DIST / exact))
                     * np.float32(REL_BUCKETS - exact)).astype(np.int32)
    large = np.minimum(large, REL_BUCKETS - 1)
    return np.where(n < exact, n, large).astype(np.int32)


def _bucket_tiles(t):
    kk = np.arange(t, dtype=np.int64)[:, None]
    qq = np.arange(t, dtype=np.int64)[None, :]
    d_diag = qq - kk
    diag = np.where(d_diag >= 0, _rel_bucket_np(d_diag), -1).astype(np.int32)
    off = _rel_bucket_np(qq + t - kk)
    assert int(_rel_bucket_np(np.arange(t + 1, 8 * t)).min()) == REL_BUCKETS - 1
    return diag, off


def _da_bias_body(rb_ref, bd_ref, bo_ref, out_ref, *, t):
    h = pl.program_id(0)
    far = rb_ref[h, REL_BUCKETS - 1]
    for which, bref in ((0, bd_ref), (1, bo_ref)):
        bk = bref[...]
        acc = jnp.zeros((t, t), F32)
        for bb in range(REL_BUCKETS - 1):
            acc = jnp.where(bk == bb, rb_ref[h, bb] - far, acc)
        out_ref[which] = jnp.where(bk < 0, NEG_BIG, acc * LOG2E)


def _da_bias(rel_bias, t):
    diag, off = _bucket_tiles(t)
    hh = rel_bias.shape[0]
    return pl.pallas_call(
        functools.partial(_da_bias_body, t=t),
        grid=(hh,),
        in_specs=[
            pl.BlockSpec(memory_space=pltpu.SMEM),
            _resident((t, t)),
            _resident((t, t)),
        ],
        out_specs=pl.BlockSpec((None, 2, t, t), lambda i: (i, 0, 0, 0)),
        out_shape=jax.ShapeDtypeStruct((hh, 2, t, t), F32),
        compiler_params=_params(("parallel",)),
        name="da_bias",
    )(rel_bias, jnp.asarray(diag), jnp.asarray(off))


def _da_attn_body(lv_ref, sub_ref, qT_ref, k_ref, vT_ref, bias_ref, o_ref,
                  qb_s, m_s, acc_s, s0_s, s1_s, mx0_s, mx1_s, *, tq, tk, lam_init):
    qi = pl.program_id(2)
    sw = ATT_STRIP
    per_half = tk // sw
    qT = qT_ref[...]
    row = lax.broadcasted_iota(jnp.int32, qT.shape, 0)
    zero = jnp.zeros_like(qT)
    qb_s[:, 0:tq] = jnp.where(row < DA_QK_DIM, qT, zero)
    qb_s[:, tq:2 * tq] = jnp.where(row >= DA_QK_DIM, qT, zero)
    m_s[...] = jnp.full(m_s.shape, NEG_BIG, F32)
    acc_s[...] = jnp.zeros(acc_s.shape, F32)

    kinds = {"far": (None, None), "pre": (1, None), "d0": (0, 1), "d1": ("skip", 0)}

    def strips(kind):
        for comp in range(2):
            for r in range(tq // sw):
                which = kinds[kind][r // per_half]
                if which == "skip":
                    continue
                qoff = (r % per_half) * sw
                nk = min(qoff + sw, tk) if which == 0 else tk
                yield slice(comp * tq + r * sw, comp * tq + (r + 1) * sw), which, qoff, nk

    def a_strip(kt, bufs, cols, which, qoff, nk):
        buf, mx = bufs
        s = _dot(kt[0:nk, :], qb_s[:, cols])
        if which is not None:
            s = s + bias_ref[which, 0:nk, qoff:qoff + sw]
        buf[0:nk, cols] = s
        mx[:, cols] = jnp.max(s, axis=0, keepdims=True)

    def b_strip(vt, bufs, cols, nk):
        buf, mx = bufs
        m_old = m_s[:, cols]
        m_new = jnp.maximum(m_old, mx[:, cols])
        alpha = jnp.exp2(m_old - m_new)
        p = jnp.exp2(buf[0:nk, cols] - m_new)
        acc_s[:, cols] = alpha * acc_s[:, cols] + _dot(vt[:, 0:nk], p.astype(BF16))
        m_s[:, cols] = m_new

    def step(a=None, b=None):
        sa, sb = [], []
        if a is not None:
            kj, kind, bufs_a = a
            kt = k_ref[pl.ds(pl.multiple_of(kj * tk, tk), tk), :]
            sa = list(strips(kind))
        if b is not None:
            kj, kind, bufs_b = b
            vt = vT_ref[:, pl.ds(pl.multiple_of(kj * tk, tk), tk)]
            vt = jnp.concatenate([vt, jnp.ones((DA_SUM_ROWS, tk), BF16)], axis=0)
            sb = list(strips(kind))
        for n in range(max(len(sa), len(sb))):
            if n < len(sa):
                a_strip(kt, bufs_a, *sa[n])
            if n < len(sb):
                b_strip(vt, bufs_b, sb[n][0], sb[n][3])

    b0, b1 = (s0_s, mx0_s), (s1_s, mx1_s)

    @pl.when(qi == 0)
    def _():
        step(a=(0, "d0", b0))
        step(a=(1, "d1", b1), b=(0, "d0", b0))
        step(b=(1, "d1", b1))

    @pl.when(qi > 0)
    def _():
        step(a=(0, "far", b0))

        def far_steps(base, count):
            for t in range(count):
                even = t % 2 == 0
                step(a=(base + t + 1, "far", b1 if even else b0), b=(base + t, "far", b0 if even else b1))

        def far4(i, carry):
            far_steps(4 * i, 4)
            return carry

        n_pairs = qi - 1
        lax.fori_loop(0, n_pairs // 2, far4, 0)

        @pl.when(n_pairs % 2 == 1)
        def _():
            far_steps(2 * (n_pairs - 1), 2)

        step(a=(2 * qi - 1, "pre", b1), b=(2 * qi - 2, "far", b0))
        step(a=(2 * qi, "d0", b0), b=(2 * qi - 1, "pre", b1))
        step(a=(2 * qi + 1, "d1", b1), b=(2 * qi, "d0", b0))
        step(b=(2 * qi + 1, "d1", b1))

    lv = lv_ref[...]
    s1 = jnp.sum(lv[0:1] * lv[1:2], axis=-1, keepdims=True)
    s2 = jnp.sum(lv[2:3] * lv[3:4], axis=-1, keepdims=True)
    lam = jnp.exp(s1) - jnp.exp(s2) + lam_init
    o2 = acc_s[0:DA_V_DIM, :] * (1.0 / acc_s[DA_V_DIM:DA_V_DIM + 1, :])
    oT = o2[:, 0:tq] - lam * o2[:, tq:2 * tq]
    o = oT.T
    o = _rms(o, sub_ref[...]) * (1.0 - lam_init)
    o_ref[...] = o.astype(BF16)


def _da_attn(lv, subln, qT, k, vT, bias, lam_init):
    b, d, s = qT.shape
    tq, tk = 2 * ATT_TILE, ATT_TILE
    hd = DA_V_DIM
    nh = d // hd
    return pl.pallas_call(
        functools.partial(_da_attn_body, tq=tq, tk=tk, lam_init=lam_init),
        grid=(b, nh, s // tq),
        in_specs=[
            _resident(lv.shape),
            _resident(subln.shape),
            pl.BlockSpec((None, hd, tq), lambda i, h, j: (i, h, j)),
            pl.BlockSpec((None, s, hd), lambda i, h, j: (i, 0, h)),
            pl.BlockSpec((None, hd, s), lambda i, h, j: (i, h, 0)),
            pl.BlockSpec((None, 2, tk, tk), lambda i, h, j: (h, 0, 0, 0)),
        ],
        out_specs=pl.BlockSpec((None, tq, hd), lambda i, h, j: (i, j, h)),
        out_shape=jax.ShapeDtypeStruct((b, s, d), BF16),
        scratch_shapes=[
            pltpu.VMEM((hd, 2 * tq), BF16),
            pltpu.VMEM((1, 2 * tq), F32),
            pltpu.VMEM((hd + DA_SUM_ROWS, 2 * tq), F32),
            pltpu.VMEM((tk, 2 * tq), F32),
            pltpu.VMEM((tk, 2 * tq), F32),
            pltpu.VMEM((1, 2 * tq), F32),
            pltpu.VMEM((1, 2 * tq), F32),
        ],
        compiler_params=_params(("parallel", "parallel", "arbitrary")),
        name="da_attn",
    )(lv, subln, qT, k, vT, bias)


def _hg_in_body(x_ref, g_ref, w_ref, lbs_ref, q_ref, k_ref, v_ref, gate_ref, lf_ref, *, d, layer):
    lbs = lbs_ref[...]
    e = jnp.exp(lbs - jnp.max(lbs, axis=0, keepdims=True))
    sm = e / jnp.sum(e, axis=0, keepdims=True)
    lb = jnp.sum(sm[0:layer + 1], axis=0, keepdims=True) - sm[0:1]
    a = jnp.log(lb)
    log_1m_lb = jnp.log(1.0 - lb)
    tm = x_ref.shape[0]
    for i in range(tm // HG_IN_ROWS):
        rows = slice(i * HG_IN_ROWS, (i + 1) * HG_IN_ROWS)
        h = _rms(x_ref[rows, :], g_ref[...]).astype(BF16)
        q = _dot(h, w_ref[:, 0:d])
        q_ref[rows, :] = _silu(q).astype(BF16)
        f = _dot(h, w_ref[:, d:2 * d])
        log_sig = -(jnp.maximum(-f, 0.0) + jnp.log(1.0 + jnp.exp(-jnp.abs(f))))
        c = log_1m_lb + log_sig
        log_f = jnp.maximum(a, c) + jnp.log(1.0 + jnp.exp(-jnp.abs(a - c)))
        lf_ref[rows, :] = log_f
        k_ref[rows, :] = (1.0 - jnp.exp(log_f)).astype(BF16)
        v_ref[rows, :] = _dot(h, w_ref[:, 2 * d:3 * d]).astype(BF16)
        gate_ref[rows, :] = _dot(h, w_ref[:, 3 * d:4 * d]).astype(BF16)


def _hg_in(x, g, w, j, lbs, layer):
    b, s, d = x.shape
    tm = TOK_TILE
    tok = pl.BlockSpec((None, tm, d), lambda i, j: (i, j, 0))
    return pl.pallas_call(
        functools.partial(_hg_in_body, d=d, layer=layer),
        grid=(b, s // tm),
        in_specs=[tok, _resident((1, d)), _layer(w, j), _resident(lbs.shape)],
        out_specs=[tok, tok, tok, tok, tok],
        out_shape=[jax.ShapeDtypeStruct((b, s, d), BF16)] * 4 + [jax.ShapeDtypeStruct((b, s, d), F32)],
        compiler_params=_params(("parallel", "parallel")),
        name="hg_in",
    )(x, g, w, lbs)


def _hg_core_body(q_ref, k_ref, v_ref, gate_ref, lf_ref, on_ref, o_ref,
                  st_s, g_s, q_s, k_s, v_s, o_s, a_s, *, tile, nh):
    c64, c16, hd = HG_CHUNK, HG_SUB, HG_DIM
    c8 = c16 // 2
    nsub = c64 // c16
    heads = [slice(h * hd, (h + 1) * hd) for h in range(nh)]

    @pl.when(pl.program_id(1) == 0)
    def _():
        st_s[...] = jnp.zeros(st_s.shape, F32)

    ri = lax.broadcasted_iota(jnp.int32, (c64, c64), 0)
    ci = lax.broadcasted_iota(jnp.int32, (c64, c64), 1)
    tri = jnp.where(ri >= ci, 1.0, 0.0).astype(BF16)
    tt = lax.broadcasted_iota(jnp.int32, (c8, 1), 0)
    on = on_ref[...]

    def chunk(c, carry, fast):
        base = pl.multiple_of(c * c64, c64)
        rows = pl.ds(base, c64)
        lf = lf_ref[rows, :]
        hi = lf.astype(BF16)
        r1 = lf - hi.astype(F32)
        mid = r1.astype(BF16)
        lo = (r1 - mid.astype(F32)).astype(BF16)
        gc = _dot(tri, hi) + _dot(tri, mid) + _dot(tri, lo)
        q = q_ref[rows, :].astype(F32)
        k = k_ref[rows, :].astype(F32)
        vb = v_ref[rows, :]
        g_s[...] = gc
        q_s[...] = q
        k_s[...] = k
        if not fast:
            v_s[...] = vb.astype(F32)
        g_last = gc[c64 - 1:c64, :]

        qg = (q * jnp.exp(gc)).astype(BF16)
        kd = (k * jnp.exp(g_last - gc)).astype(BF16)
        dec = jnp.exp(g_last)

        a_s[...] = jnp.zeros(a_s.shape, BF16)
        for i in range(0 if fast else 1, nsub):
            lo_r, hi_r = i * c16, (i + 1) * c16
            nk = hi_r if fast else lo_r
            if i == 0:
                qi = q_s[lo_r:hi_r, :] * jnp.exp(g_s[lo_r:hi_r, :])
                kj = k_s[0:nk, :] * jnp.exp(-g_s[0:nk, :])
            else:
                ref = g_s[lo_r - 1:lo_r, :]
                qi = q_s[lo_r:hi_r, :] * jnp.exp(g_s[lo_r:hi_r, :] - ref)
                kj = k_s[0:nk, :] * jnp.exp(ref - g_s[0:nk, :])
            qi = qi.astype(BF16)
            kj = kj.astype(BF16)
            if fast:
                keep = (lax.broadcasted_iota(jnp.int32, (c16, nk), 1)
                        <= lax.broadcasted_iota(jnp.int32, (c16, nk), 0) + lo_r)
            for h, cols in enumerate(heads):
                a = _dot_nt(qi[:, cols], kj[:, cols])
                if fast:
                    a = jnp.where(keep, a, 0.0)
                a_s[h, lo_r:hi_r, 0:nk] = a.astype(BF16)

        for h, cols in enumerate(heads):
            st = st_s[h]
            o_s[:, cols] = _dot_nt(qg[:, cols], st.astype(BF16)) + _dot(a_s[h], vb[:, cols])
            st_s[h] = st * dec[:, cols] + _dot_tn(vb[:, cols], kd[:, cols])

        grp = max(nh // 2, 1)
        if not fast:
            for i in range(nsub):
                for g0 in range(0, nh, grp):
                    wide = slice(g0 * hd, (g0 + grp) * hd)
                    sub = [slice(h * hd, (h + 1) * hd) for h in range(grp)]
                    r0, r1_, r2 = i * c16, i * c16 + c8, (i + 1) * c16
                    g_lo, g_hi = g_s[r0:r1_, wide], g_s[r1_:r2, wide]
                    q_lo, q_hi = q_s[r0:r1_, wide], q_s[r1_:r2, wide]
                    acc_lo = [None] * grp
                    acc_hi = [None] * grp
                    for s in range(c16):
                        j = r0 + s
                        gs, ks, vs = g_s[j:j + 1, wide], k_s[j:j + 1, wide], v_s[j:j + 1, wide]
                        pr_hi = q_hi * ks * jnp.exp(g_hi - gs)
                        if s < c8:
                            pr_lo = q_lo * ks * jnp.exp(g_lo - gs)
                        for h, cols in enumerate(sub):
                            a = jnp.sum(pr_hi[:, cols], axis=-1, keepdims=True)
                            if s >= c8:
                                a = jnp.where(tt >= s - c8, a, 0.0)
                            t_hi = a * vs[:, cols]
                            acc_hi[h] = t_hi if acc_hi[h] is None else acc_hi[h] + t_hi
                            if s < c8:
                                a = jnp.sum(pr_lo[:, cols], axis=-1, keepdims=True)
                                a = jnp.where(tt >= s, a, 0.0)
                                t_lo = a * vs[:, cols]
                                acc_lo[h] = t_lo if acc_lo[h] is None else acc_lo[h] + t_lo
                    for h in range(grp):
                        cols = slice((g0 + h) * hd, (g0 + h + 1) * hd)
                        o_s[r0:r1_, cols] += acc_lo[h]
                        o_s[r1_:r2, cols] += acc_hi[h]

        gate = gate_ref[rows, :].astype(F32)
        sg = _silu(gate)
        for cols in heads:
            o = _rms(o_s[:, cols], on)
            o_ref[rows, cols] = (o * sg[:, cols]).astype(BF16)
        return carry

    fast = jnp.min(lf_ref[...]) >= -HG_MAX_EXPONENT / c16

    @pl.when(fast)
    def _():
        lax.fori_loop(0, tile // c64, functools.partial(chunk, fast=True), 0, unroll=4)

    @pl.when(jnp.logical_not(fast))
    def _():
        lax.fori_loop(0, tile // c64, functools.partial(chunk, fast=False), 0)


def _hg_core(q, k, v, gate, lf, onorm):
    b, s, d = q.shape
    hd = HG_DIM
    nh = d // hd
    tile = HG_TILE
    blk = pl.BlockSpec((None, tile, d), lambda i, j: (i, j, 0))
    return pl.pallas_call(
        functools.partial(_hg_core_body, tile=tile, nh=nh),
        grid=(b, s // tile),
        in_specs=[blk, blk, blk, blk, blk, _resident(onorm.shape)],
        out_specs=blk,
        out_shape=jax.ShapeDtypeStruct((b, s, d), BF16),
        scratch_shapes=[
            pltpu.VMEM((nh, hd, hd), F32),
            pltpu.VMEM((HG_CHUNK, d), F32),
            pltpu.VMEM((HG_CHUNK, d), F32),
            pltpu.VMEM((HG_CHUNK, d), F32),
            pltpu.VMEM((HG_CHUNK, d), F32),
            pltpu.VMEM((HG_CHUNK, d), F32),
            pltpu.VMEM((nh, HG_CHUNK, HG_CHUNK), BF16),
        ],
        compiler_params=_params(("parallel", "arbitrary")),
        name="hg_core",
    )(q, k, v, gate, lf, onorm)


def _gelu(x):
    return 0.5 * x * (1.0 + lax.erf(x * (1.0 / math.sqrt(2.0))))


def _sgu_body(x_ref, g_ref, win_ref, vn_ref, ws_ref, bsT_ref, wout_ref, o_ref, a_s, *, d, tm):
    x = x_ref[...]
    h = _rms(x, g_ref[...]).astype(BF16)
    v = _gelu(_dot(h, win_ref[:, d:2 * d]))
    v = _rms(v, vn_ref[...]).astype(BF16)
    gd = d // SG_GROUPS
    ri = lax.broadcasted_iota(jnp.int32, (SG_CHUNK, SG_CHUNK), 0)
    ci = lax.broadcasted_iota(jnp.int32, (SG_CHUNK, SG_CHUNK), 1)
    bsT = bsT_ref[...]
    for pair in range(SG_GROUPS // 2):
        u2 = _gelu(_dot(h, win_ref[:, 2 * pair * gd:2 * (pair + 1) * gd]))
        for half in range(2):
            gidx = 2 * pair + half
            cols = slice(gidx * gd, (gidx + 1) * gd)
            w = jnp.where(ri >= ci, ws_ref[gidx], 0.0).astype(BF16)
            bias = bsT[:, gidx:gidx + 1]
            for c in range(tm // SG_CHUNK):
                rows = slice(c * SG_CHUNK, (c + 1) * SG_CHUNK)
                mixed = _dot(w, v[rows, cols]) + bias
                a_s[rows, cols] = (u2[rows, half * gd:(half + 1) * gd] * mixed).astype(BF16)
    o_ref[...] = x + _dot(a_s[...], wout_ref[...])


def _sgu(x, g, w_in, vnorm, w_s, b_sT, w_out, j):
    b, s, d = x.shape
    tm = TOK_TILE
    tok = pl.BlockSpec((None, tm, d), lambda i, j: (i, j, 0))
    return pl.pallas_call(
        functools.partial(_sgu_body, d=d, tm=tm),
        grid=(b, s // tm),
        in_specs=[tok, _resident((1, d)), _layer(w_in, j), _resident((1, d)),
                  _resident(w_s.shape), _resident(b_sT.shape), _layer(w_out, j)],
        out_specs=tok,
        out_shape=jax.ShapeDtypeStruct((b, s, d), F32),
        scratch_shapes=[pltpu.VMEM((tm, d), BF16)],
        compiler_params=_params(("parallel", "parallel")),
        name="sgu",
    )(x, g, w_in, vnorm, w_s, b_sT, w_out)


def _kv_mem_body(mem_ref, g_ref, w_ref, k_ref, v_ref, *, d):
    h = _rms(mem_ref[...], g_ref[...]).astype(BF16)
    k_ref[...] = _dot(h, w_ref[:, 0:d]).astype(BF16)
    v_ref[...] = _dot(h, w_ref[:, d:2 * d]).astype(BF16)


def _kv_mem(mem, g, w_kv):
    b, m, d = mem.shape
    depth = w_kv.shape[0]
    return pl.pallas_call(
        functools.partial(_kv_mem_body, d=d),
        grid=(depth, b),
        in_specs=[
            pl.BlockSpec((None, m, d), lambda l, i: (i, 0, 0)),
            _resident((1, d)),
            pl.BlockSpec((None, d, 2 * d), lambda l, i: (l, 0, 0)),
        ],
        out_specs=[
            pl.BlockSpec((None, None, m, d), lambda l, i: (l, i, 0, 0)),
            pl.BlockSpec((None, None, m, d), lambda l, i: (l, i, 0, 0)),
        ],
        out_shape=[
            jax.ShapeDtypeStruct((depth, b, m, d), BF16),
            jax.ShapeDtypeStruct((depth, b, m, d), BF16),
        ],
        compiler_params=_params(("parallel", "parallel")),
        name="kv_mem",
    )(mem, g, w_kv)


def _cross_body(*refs, d, q_scale, fused_proj):
    if fused_proj:
        x_ref, ap_ref, wp_ref, g_ref, wq_ref, k_ref, v_ref, wo_ref, o_ref, a_s, q_s, x_s = refs
    else:
        x_ref, g_ref, wq_ref, k_ref, v_ref, wo_ref, o_ref, a_s, q_s, x_s = refs
    hd = d // CA_HEADS
    tm = x_ref.shape[0]
    parts = [slice(i * CA_ROWS, (i + 1) * CA_ROWS) for i in range(tm // CA_ROWS)]
    for rows in parts:
        x = x_ref[rows, :]
        if fused_proj:
            x = x + _dot(ap_ref[rows, :], wp_ref[...])
        x_s[rows, :] = x
        h = _rms(x, g_ref[...]).astype(BF16)
        q_s[rows, :] = (_dot(h, wq_ref[...]) * q_scale).astype(BF16)
    for rows in parts:
        for hh in range(CA_HEADS):
            cols = slice(hh * hd, (hh + 1) * hd)
            s = _dot_nt(q_s[rows, cols], k_ref[:, cols])
            m = jnp.max(s, axis=-1, keepdims=True)
            p = jnp.exp2(s - m)
            l = jnp.sum(p, axis=-1, keepdims=True)
            o = _dot(p.astype(BF16), v_ref[:, cols]) * (1.0 / l)
            a_s[rows, cols] = o.astype(BF16)
    for rows in parts:
        o_ref[rows, :] = x_s[rows, :] + _dot(a_s[rows, :], wo_ref[...])


def _cross(x, g, wq, k, v, wo, layer, proj=None):
    b, s, d = x.shape
    m = v.shape[2]
    tm = 2 * TOK_TILE
    tok = pl.BlockSpec((None, tm, d), lambda i, j: (i, j, 0))
    q_scale = (d // CA_HEADS) ** -0.5 * LOG2E
    pre_specs, pre_args = ([tok, _layer(proj[1], proj[2])], list(proj[:2])) if proj is not None else ([], [])
    return pl.pallas_call(
        functools.partial(_cross_body, d=d, q_scale=q_scale, fused_proj=proj is not None),
        grid=(b, s // tm),
        in_specs=[tok] + pre_specs + [
            _resident((1, d)), _layer(wq, layer),
            pl.BlockSpec((None, None, m, d), lambda i, j: (layer, i, 0, 0)),
            pl.BlockSpec((None, None, m, d), lambda i, j: (layer, i, 0, 0)),
            _layer(wo, layer),
        ],
        out_specs=tok,
        out_shape=jax.ShapeDtypeStruct((b, s, d), F32),
        scratch_shapes=[pltpu.VMEM((tm, d), BF16), pltpu.VMEM((tm, d), BF16), pltpu.VMEM((tm, d), F32)],
        compiler_params=_params(("parallel", "parallel")),
        name="cross",
    )(x, *pre_args, g, wq, k, v, wo)


def _ffn_body(x_ref, g_ref, wgu_ref, wd_ref, gf_ref, o_ref, a_s, *, ff, final_norm):
    x = x_ref[...]
    h = _rms(x, g_ref[...]).astype(BF16)
    fc = FFN_CHUNK
    for c in range(ff // fc):
        gate = _dot(h, wgu_ref[:, c * fc:(c + 1) * fc])
        up = _dot(h, wgu_ref[:, ff + c * fc:ff + (c + 1) * fc])
        a_s[:, c * fc:(c + 1) * fc] = (_silu(gate) * up).astype(BF16)
    y = x + _dot(a_s[...], wd_ref[...])
    if final_norm:
        y = _rms(y, gf_ref[...])
    o_ref[...] = y


def _ffn(x, g, wgu, wd, layer, gf, final_norm):
    b, s, d = x.shape
    ff = wd.shape[1]
    tm = TOK_TILE
    tok = pl.BlockSpec((None, tm, d), lambda i, j: (i, j, 0))
    return pl.pallas_call(
        functools.partial(_ffn_body, ff=ff, final_norm=final_norm),
        grid=(b, s // tm),
        in_specs=[tok, _resident((1, d)), _layer(wgu, layer), _layer(wd, layer), _resident((1, d))],
        out_specs=tok,
        out_shape=jax.ShapeDtypeStruct((b, s, d), F32),
        scratch_shapes=[pltpu.VMEM((tm, ff), BF16)],
        compiler_params=_params(("parallel", "parallel")),
        name="ffn",
    )(x, g, wgu, wd, gf)


def kernel(x, mem, rel_bias, norm_mix, norm_cross, norm_ffn, norm_mem, norm_final,
           da_w_in, da_w_out, da_lq1, da_lk1, da_lq2, da_lk2, da_subln,
           hg_w_in, hg_w_out, hg_lower_bounds, hg_onorm,
           sg_w_in, sg_w_out, sg_vnorm, sg_w_s, sg_b_s,
           ca_w_q, ca_w_kv, ca_w_o, ffn_w_gu, ffn_w_down):
    depth = norm_mix.shape[0]
    row = lambda vec: vec.reshape(1, -1)
    bf = lambda w: w.astype(BF16)

    k_mem, v_mem = _kv_mem(mem, row(norm_mem), bf(ca_w_kv))
    bias = _da_bias(rel_bias, ATT_TILE)
    da_scale = DA_QK_DIM ** -0.5 * LOG2E
    da_w_in, da_w_out, hg_w_in, hg_w_out = bf(da_w_in), bf(da_w_out), bf(hg_w_in), bf(hg_w_out)
    sg_w_in, sg_w_out = bf(sg_w_in), bf(sg_w_out)
    ca_w_q, ca_w_o, ffn_w_gu, ffn_w_down = bf(ca_w_q), bf(ca_w_o), bf(ffn_w_gu), bf(ffn_w_down)

    for i in range(depth):
        kind, j = i % N_MIXERS, i // N_MIXERS
        g_mix = row(norm_mix[i])
        if kind == 0:
            lam_init = 0.8 - 0.6 * math.exp(-0.3 * i)
            qT, k, vT = _da_in(x, g_mix, da_w_in, j, da_scale)
            lv = jnp.stack([da_lq1[j], da_lk1[j], da_lq2[j], da_lk2[j]])
            a = _da_attn(lv, row(da_subln[j]), qT, k, vT, bias, lam_init)
            proj = (a, da_w_out, j)
        elif kind == 1:
            q, k, v, gate, lf = _hg_in(x, g_mix, hg_w_in, j, hg_lower_bounds, i)
            a = _hg_core(q, k, v, gate, lf, row(hg_onorm[j]))
            proj = (a, hg_w_out, j)
        else:
            x = _sgu(x, g_mix, sg_w_in, row(sg_vnorm[j]), sg_w_s[j], sg_b_s[j].T, sg_w_out, j)
            proj = None
        x = _cross(x, row(norm_cross[i]), ca_w_q, k_mem, v_mem, ca_w_o, i, proj)
        x = _ffn(x, row(norm_ffn[i]), ffn_w_gu, ffn_w_down, i, row(norm_final), i == depth - 1)
    return x
```

```python
import functools
import math

import numpy as np
import jax
import jax.numpy as jnp
from jax import lax
from jax.experimental import pallas as pl
from jax.experimental.pallas import tpu as pltpu

F32 = jnp.float32
BF16 = jnp.bfloat16

EPS = 1e-6
LOG2E = math.log2(math.e)
NEG_BIG = -1e30

DA_HEADS = 8
DA_QK_DIM = 64
DA_V_DIM = 128
DA_SUM_ROWS = 16
REL_BUCKETS = 32
REL_MAX_DIST = 128
HG_HEADS = 8
HG_DIM = 128
HG_CHUNK = 64
HG_SUB = 16
HG_MAX_EXPONENT = 80.0
SG_CHUNK = 128
SG_GROUPS = 8
CA_HEADS = 4
N_MIXERS = 3

VMEM_LIMIT = 56 * 1024 * 1024
TOK_TILE = 512
ATT_TILE = 512
ATT_STRIP = 256
HG_TILE = 512
FFN_CHUNK = 256
CA_ROWS = 256
HG_IN_ROWS = 256


def _params(sem):
    return pltpu.CompilerParams(dimension_semantics=sem, vmem_limit_bytes=VMEM_LIMIT)


def _resident(shape):
    nd = len(shape)
    return pl.BlockSpec(shape, lambda *_: (0,) * nd, pipeline_mode=pl.Buffered(1))


def _layer(w, layer):
    nd = w.ndim - 1
    return pl.BlockSpec((None,) + w.shape[1:], lambda *_: (layer,) + (0,) * nd, pipeline_mode=pl.Buffered(1))


def _rms(x, g):
    ms = jnp.mean(x * x, axis=-1, keepdims=True)
    return x * lax.rsqrt(ms + EPS) * g


def _dot(a, b):
    return jnp.dot(a, b, preferred_element_type=F32)


def _dot_nt(a, b):
    return lax.dot_general(a, b, (((1,), (1,)), ((), ())), preferred_element_type=F32)


def _dot_tn(a, b):
    return lax.dot_general(a, b, (((0,), (0,)), ((), ())), preferred_element_type=F32)


def _silu(x):
    return x * (1.0 / (1.0 + jnp.exp(-x)))


def _da_in_body(x_ref, g_ref, w_ref, qT_ref, k_ref, vT_ref, t_s, *, d, q_scale):
    h = _rms(x_ref[...], g_ref[...]).astype(BF16)
    t_s[...] = _dot(h, w_ref[:, 0:d]) * q_scale
    qT_ref[...] = t_s[...].T.astype(BF16)
    k_ref[...] = _dot(h, w_ref[:, d:2 * d]).astype(BF16)
    t_s[...] = _dot(h, w_ref[:, 2 * d:3 * d])
    vT_ref[...] = t_s[...].T.astype(BF16)


def _da_in(x, g, w, j, q_scale):
    b, s, d = x.shape
    tm = 2 * TOK_TILE
    return pl.pallas_call(
        functools.partial(_da_in_body, d=d, q_scale=q_scale),
        grid=(b, s // tm),
        in_specs=[
            pl.BlockSpec((None, tm, d), lambda i, j: (i, j, 0)),
            _resident((1, d)),
            _layer(w, j),
        ],
        out_specs=[
            pl.BlockSpec((None, d, tm), lambda i, j: (i, 0, j)),
            pl.BlockSpec((None, tm, d), lambda i, j: (i, j, 0)),
            pl.BlockSpec((None, d, tm), lambda i, j: (i, 0, j)),
        ],
        out_shape=[
            jax.ShapeDtypeStruct((b, d, s), BF16),
            jax.ShapeDtypeStruct((b, s, d), BF16),
            jax.ShapeDtypeStruct((b, d, s), BF16),
        ],
        scratch_shapes=[pltpu.VMEM((tm, d), F32)],
        compiler_params=_params(("parallel", "parallel")),
        name="da_in",
    )(x, g, w)


def _rel_bucket_np(dist):
    n = np.maximum(dist, 0)
    exact = REL_BUCKETS // 2
    nf = np.maximum(n, exact).astype(np.float32)
    large = exact + (np.log(nf / np.float32(exact)) / np.float32(math.log(REL_MAX_DIST / exact))
                     * np.float32(REL_BUCKETS - exact)).astype(np.int32)
    large = np.minimum(large, REL_BUCKETS - 1)
    return np.where(n < exact, n, large).astype(np.int32)


def _bucket_tiles(t):
    kk = np.arange(t, dtype=np.int64)[:, None]
    qq = np.arange(t, dtype=np.int64)[None, :]
    d_diag = qq - kk
    diag = np.where(d_diag >= 0, _rel_bucket_np(d_diag), -1).astype(np.int32)
    off = _rel_bucket_np(qq + t - kk)
    assert int(_rel_bucket_np(np.arange(t + 1, 8 * t)).min()) == REL_BUCKETS - 1
    return diag, off


def _da_bias_body(rb_ref, bd_ref, bo_ref, out_ref, *, t):
    h = pl.program_id(0)
    far = rb_ref[h, REL_BUCKETS - 1]
    for which, bref in ((0, bd_ref), (1, bo_ref)):
        bk = bref[...]
        acc = jnp.zeros((t, t), F32)
        for bb in range(REL_BUCKETS - 1):
            acc = jnp.where(bk == bb, rb_ref[h, bb] - far, acc)
        out_ref[which] = jnp.where(bk < 0, NEG_BIG, acc * LOG2E)


def _da_bias(rel_bias, t):
    diag, off = _bucket_tiles(t)
    hh = rel_bias.shape[0]
    return pl.pallas_call(
        functools.partial(_da_bias_body, t=t),
        grid=(hh,),
        in_specs=[
            pl.BlockSpec(memory_space=pltpu.SMEM),
            _resident((t, t)),
            _resident((t, t)),
        ],
        out_specs=pl.BlockSpec((None, 2, t, t), lambda i: (i, 0, 0, 0)),
        out_shape=jax.ShapeDtypeStruct((hh, 2, t, t), F32),
        compiler_params=_params(("parallel",)),
        name="da_bias",
    )(rel_bias, jnp.asarray(diag), jnp.asarray(off))


def _da_attn_body(lv_ref, sub_ref, qT_ref, k_ref, vT_ref, bias_ref, o_ref,
                  qb_s, m_s, acc_s, s0_s, s1_s, mx0_s, mx1_s, *, tq, tk, nq, lam_init):
    sw = ATT_STRIP
    per_half = tk // sw

    def load_q(qi):
        qT = qT_ref[:, pl.ds(pl.multiple_of(qi * tq, tq), tq)]
        row = lax.broadcasted_iota(jnp.int32, qT.shape, 0)
        zero = jnp.zeros_like(qT)
        qb_s[:, 0:tq] = jnp.where(row < DA_QK_DIM, qT, zero)
        qb_s[:, tq:2 * tq] = jnp.where(row >= DA_QK_DIM, qT, zero)

    def reset():
        m_s[...] = jnp.full(m_s.shape, NEG_BIG, F32)
        acc_s[...] = jnp.zeros(acc_s.shape, F32)

    lv = lv_ref[...]
    s1 = jnp.sum(lv[0:1] * lv[1:2], axis=-1, keepdims=True)
    s2 = jnp.sum(lv[2:3] * lv[3:4], axis=-1, keepdims=True)
    lam = jnp.exp(s1) - jnp.exp(s2) + lam_init

    def finalize(qi):
        o2 = acc_s[0:DA_V_DIM, :] * (1.0 / acc_s[DA_V_DIM:DA_V_DIM + 1, :])
        oT = o2[:, 0:tq] - lam * o2[:, tq:2 * tq]
        o = _rms(oT.T, sub_ref[...]) * (1.0 - lam_init)
        o_ref[pl.ds(pl.multiple_of(qi * tq, tq), tq), :] = o.astype(BF16)

    kinds = {"far": (None, None), "pre": (1, None), "d0": (0, 1), "d1": ("skip", 0)}

    def strips(kind):
        for comp in range(2):
            for r in range(tq // sw):
                which = kinds[kind][r // per_half]
                if which == "skip":
                    continue
                qoff = (r % per_half) * sw
                nk = min(qoff + sw, tk) if which == 0 else tk
                yield slice(comp * tq + r * sw, comp * tq + (r + 1) * sw), which, qoff, nk

    def a_strip(kt, bufs, cols, which, qoff, nk):
        buf, mx = bufs
        s = _dot(kt[0:nk, :], qb_s[:, cols])
        if which is not None:
            s = s + bias_ref[which, 0:nk, qoff:qoff + sw]
        buf[0:nk, cols] = s
        mx[:, cols] = jnp.max(s, axis=0, keepdims=True)

    def b_strip(vt, bufs, cols, nk):
        buf, mx = bufs
        m_old = m_s[:, cols]
        m_new = jnp.maximum(m_old, mx[:, cols])
        alpha = jnp.exp2(m_old - m_new)
        p = jnp.exp2(buf[0:nk, cols] - m_new)
        acc_s[:, cols] = alpha * acc_s[:, cols] + _dot(vt[:, 0:nk], p.astype(BF16))
        m_s[:, cols] = m_new

    def step(a=None, b=None):
        sa, sb = [], []
        if a is not None:
            kj, kind, bufs_a = a
            kt = k_ref[pl.ds(pl.multiple_of(kj * tk, tk), tk), :]
            sa = list(strips(kind))
        if b is not None:
            kj, kind, bufs_b = b
            vt = vT_ref[:, pl.ds(pl.multiple_of(kj * tk, tk), tk)]
            vt = jnp.concatenate([vt, jnp.ones((DA_SUM_ROWS, tk), BF16)], axis=0)
            sb = list(strips(kind))
        for n in range(max(len(sa), len(sb))):
            if n < len(sa):
                a_strip(kt, bufs_a, *sa[n])
            if n < len(sb):
                b_strip(vt, bufs_b, sb[n][0], sb[n][3])

    b0, b1 = (s0_s, mx0_s), (s1_s, mx1_s)

    load_q(0)
    reset()
    step(a=(0, "d0", b0))
    step(a=(1, "d1", b1), b=(0, "d0", b0))

    def far_steps(base, count):
        for t in range(count):
            even = t % 2 == 0
            step(a=(base + t + 1, "far", b1 if even else b0), b=(base + t, "far", b0 if even else b1))

    def far4(i, carry):
        far_steps(4 * i, 4)
        return carry

    def q_tile(qi, carry):
        load_q(qi)
        step(a=(0, "far", b0), b=(2 * qi - 1, "d1", b1))
        finalize(qi - 1)
        reset()
        n_pairs = qi - 1
        lax.fori_loop(0, n_pairs // 2, far4, 0)

        @pl.when(n_pairs % 2 == 1)
        def _():
            far_steps(2 * (n_pairs - 1), 2)

        step(a=(2 * qi - 1, "pre", b1), b=(2 * qi - 2, "far", b0))
        step(a=(2 * qi, "d0", b0), b=(2 * qi - 1, "pre", b1))
        step(a=(2 * qi + 1, "d1", b1), b=(2 * qi, "d0", b0))
        return carry

    lax.fori_loop(1, nq, q_tile, 0)
    step(b=(2 * nq - 1, "d1", b1))
    finalize(nq - 1)


def _da_attn(lv, subln, qT, k, vT, bias, lam_init):
    b, d, s = qT.shape
    tq, tk = 2 * ATT_TILE, ATT_TILE
    hd = DA_V_DIM
    nh = d // hd
    return pl.pallas_call(
        functools.partial(_da_attn_body, tq=tq, tk=tk, nq=s // tq, lam_init=lam_init),
        grid=(b, nh),
        in_specs=[
            _resident(lv.shape),
            _resident(subln.shape),
            pl.BlockSpec((None, hd, s), lambda i, h: (i, h, 0)),
            pl.BlockSpec((None, s, hd), lambda i, h: (i, 0, h)),
            pl.BlockSpec((None, hd, s), lambda i, h: (i, h, 0)),
            pl.BlockSpec((None, 2, tk, tk), lambda i, h: (h, 0, 0, 0)),
        ],
        out_specs=pl.BlockSpec((None, s, hd), lambda i, h: (i, 0, h)),
        out_shape=jax.ShapeDtypeStruct((b, s, d), BF16),
        scratch_shapes=[
            pltpu.VMEM((hd, 2 * tq), BF16),
            pltpu.VMEM((1, 2 * tq), F32),
            pltpu.VMEM((hd + DA_SUM_ROWS, 2 * tq), F32),
            pltpu.VMEM((tk, 2 * tq), F32),
            pltpu.VMEM((tk, 2 * tq), F32),
            pltpu.VMEM((1, 2 * tq), F32),
            pltpu.VMEM((1, 2 * tq), F32),
        ],
        compiler_params=_params(("parallel", "parallel")),
        name="da_attn",
    )(lv, subln, qT, k, vT, bias)


def _hg_in_body(x_ref, g_ref, w_ref, lbs_ref, q_ref, k_ref, v_ref, gate_ref, lf_ref, *, d, layer):
    lbs = lbs_ref[...]
    e = jnp.exp(lbs - jnp.max(lbs, axis=0, keepdims=True))
    sm = e / jnp.sum(e, axis=0, keepdims=True)
    lb = jnp.sum(sm[0:layer + 1], axis=0, keepdims=True) - sm[0:1]
    a = jnp.log(lb)
    log_1m_lb = jnp.log(1.0 - lb)
    tm = x_ref.shape[0]
    for i in range(tm // HG_IN_ROWS):
        rows = slice(i * HG_IN_ROWS, (i + 1) * HG_IN_ROWS)
        h = _rms(x_ref[rows, :], g_ref[...]).astype(BF16)
        q = _dot(h, w_ref[:, 0:d])
        q_ref[rows, :] = _silu(q).astype(BF16)
        f = _dot(h, w_ref[:, d:2 * d])
        log_sig = -(jnp.maximum(-f, 0.0) + jnp.log(1.0 + jnp.exp(-jnp.abs(f))))
        c = log_1m_lb + log_sig
        log_f = jnp.maximum(a, c) + jnp.log(1.0 + jnp.exp(-jnp.abs(a - c)))
        lf_ref[rows, :] = log_f
        k_ref[rows, :] = (1.0 - jnp.exp(log_f)).astype(BF16)
        v_ref[rows, :] = _dot(h, w_ref[:, 2 * d:3 * d]).astype(BF16)
        gate_ref[rows, :] = _dot(h, w_ref[:, 3 * d:4 * d]).astype(BF16)


def _hg_in(x, g, w, j, lbs, layer):
    b, s, d = x.shape
    tm = 2 * TOK_TILE
    tok = pl.BlockSpec((None, tm, d), lambda i, j: (i, j, 0))
    return pl.pallas_call(
        functools.partial(_hg_in_body, d=d, layer=layer),
        grid=(b, s // tm),
        in_specs=[tok, _resident((1, d)), _layer(w, j), _resident(lbs.shape)],
        out_specs=[tok, tok, tok, tok, tok],
        out_shape=[jax.ShapeDtypeStruct((b, s, d), BF16)] * 4 + [jax.ShapeDtypeStruct((b, s, d), F32)],
        compiler_params=_params(("parallel", "parallel")),
        name="hg_in",
    )(x, g, w, lbs)


def _hg_core_body(q_ref, k_ref, v_ref, gate_ref, lf_ref, on_ref, o_ref,
                  st_s, g_s, q_s, k_s, v_s, o_s, a_s, *, tile, nh):
    c64, c16, hd = HG_CHUNK, HG_SUB, HG_DIM
    c8 = c16 // 2
    nsub = c64 // c16
    heads = [slice(h * hd, (h + 1) * hd) for h in range(nh)]

    @pl.when(pl.program_id(1) == 0)
    def _():
        st_s[...] = jnp.zeros(st_s.shape, F32)

    ri = lax.broadcasted_iota(jnp.int32, (c64, c64), 0)
    ci = lax.broadcasted_iota(jnp.int32, (c64, c64), 1)
    tri = jnp.where(ri >= ci, 1.0, 0.0).astype(BF16)
    tt = lax.broadcasted_iota(jnp.int32, (c8, 1), 0)
    on = on_ref[...]

    def chunk(c, carry, fast):
        base = pl.multiple_of(c * c64, c64)
        rows = pl.ds(base, c64)
        lf = lf_ref[rows, :]
        hi = lf.astype(BF16)
        r1 = lf - hi.astype(F32)
        mid = r1.astype(BF16)
        lo = (r1 - mid.astype(F32)).astype(BF16)
        gc = _dot(tri, hi) + _dot(tri, mid) + _dot(tri, lo)
        q = q_ref[rows, :].astype(F32)
        k = k_ref[rows, :].astype(F32)
        vb = v_ref[rows, :]
        g_s[...] = gc
        q_s[...] = q
        k_s[...] = k
        if not fast:
            v_s[...] = vb.astype(F32)
        g_last = gc[c64 - 1:c64, :]

        qg = (q * jnp.exp(gc)).astype(BF16)
        kd = (k * jnp.exp(g_last - gc)).astype(BF16)
        dec = jnp.exp(g_last)

        a_s[...] = jnp.zeros(a_s.shape, BF16)
        for i in range(0 if fast else 1, nsub):
            lo_r, hi_r = i * c16, (i + 1) * c16
            nk = hi_r if fast else lo_r
            if i == 0:
                qi = q_s[lo_r:hi_r, :] * jnp.exp(g_s[lo_r:hi_r, :])
                kj = k_s[0:nk, :] * jnp.exp(-g_s[0:nk, :])
            else:
                ref = g_s[lo_r - 1:lo_r, :]
                qi = q_s[lo_r:hi_r, :] * jnp.exp(g_s[lo_r:hi_r, :] - ref)
                kj = k_s[0:nk, :] * jnp.exp(ref - g_s[0:nk, :])
            qi = qi.astype(BF16)
            kj = kj.astype(BF16)
            if fast:
                keep = (lax.broadcasted_iota(jnp.int32, (c16, nk), 1)
                        <= lax.broadcasted_iota(jnp.int32, (c16, nk), 0) + lo_r)
            for h, cols in enumerate(heads):
                a = _dot_nt(qi[:, cols], kj[:, cols])
                if fast:
                    a = jnp.where(keep, a, 0.0)
                a_s[h, lo_r:hi_r, 0:nk] = a.astype(BF16)

        for h, cols in enumerate(heads):
            st = st_s[h]
            o_s[:, cols] = _dot_nt(qg[:, cols], st.astype(BF16)) + _dot(a_s[h], vb[:, cols])
            st_s[h] = st * dec[:, cols] + _dot_tn(vb[:, cols], kd[:, cols])

        grp = max(nh // 2, 1)
        if not fast:
            for i in range(nsub):
                for g0 in range(0, nh, grp):
                    wide = slice(g0 * hd, (g0 + grp) * hd)
                    sub = [slice(h * hd, (h + 1) * hd) for h in range(grp)]
                    r0, r1_, r2 = i * c16, i * c16 + c8, (i + 1) * c16
                    g_lo, g_hi = g_s[r0:r1_, wide], g_s[r1_:r2, wide]
                    q_lo, q_hi = q_s[r0:r1_, wide], q_s[r1_:r2, wide]
                    acc_lo = [None] * grp
                    acc_hi = [None] * grp
                    for s in range(c16):
                        j = r0 + s
                        gs, ks, vs = g_s[j:j + 1, wide], k_s[j:j + 1, wide], v_s[j:j + 1, wide]
                        pr_hi = q_hi * ks * jnp.exp(g_hi - gs)
                        if s < c8:
                            pr_lo = q_lo * ks * jnp.exp(g_lo - gs)
                        for h, cols in enumerate(sub):
                            a = jnp.sum(pr_hi[:, cols], axis=-1, keepdims=True)
                            if s >= c8:
                                a = jnp.where(tt >= s - c8, a, 0.0)
                            t_hi = a * vs[:, cols]
                            acc_hi[h] = t_hi if acc_hi[h] is None else acc_hi[h] + t_hi
                            if s < c8:
                                a = jnp.sum(pr_lo[:, cols], axis=-1, keepdims=True)
                                a = jnp.where(tt >= s, a, 0.0)
                                t_lo = a * vs[:, cols]
                                acc_lo[h] = t_lo if acc_lo[h] is None else acc_lo[h] + t_lo
                    for h in range(grp):
                        cols = slice((g0 + h) * hd, (g0 + h + 1) * hd)
                        o_s[r0:r1_, cols] += acc_lo[h]
                        o_s[r1_:r2, cols] += acc_hi[h]

        gate = gate_ref[rows, :].astype(F32)
        sg = _silu(gate)
        for cols in heads:
            o = _rms(o_s[:, cols], on)
            o_ref[rows, cols] = (o * sg[:, cols]).astype(BF16)
        return carry

    fast = jnp.min(lf_ref[...]) >= -HG_MAX_EXPONENT / c16

    @pl.when(fast)
    def _():
        lax.fori_loop(0, tile // c64, functools.partial(chunk, fast=True), 0, unroll=4)

    @pl.when(jnp.logical_not(fast))
    def _():
        lax.fori_loop(0, tile // c64, functools.partial(chunk, fast=False), 0)


def _hg_core(q, k, v, gate, lf, onorm):
    b, s, d = q.shape
    hd = HG_DIM
    nh = d // hd
    tile = HG_TILE
    blk = pl.BlockSpec((None, tile, d), lambda i, j: (i, j, 0))
    return pl.pallas_call(
        functools.partial(_hg_core_body, tile=tile, nh=nh),
        grid=(b, s // tile),
        in_specs=[blk, blk, blk, blk, blk, _resident(onorm.shape)],
        out_specs=blk,
        out_shape=jax.ShapeDtypeStruct((b, s, d), BF16),
        scratch_shapes=[
            pltpu.VMEM((nh, hd, hd), F32),
            pltpu.VMEM((HG_CHUNK, d), F32),
            pltpu.VMEM((HG_CHUNK, d), F32),
            pltpu.VMEM((HG_CHUNK, d), F32),
            pltpu.VMEM((HG_CHUNK, d), F32),
            pltpu.VMEM((HG_CHUNK, d), F32),
            pltpu.VMEM((nh, HG_CHUNK, HG_CHUNK), BF16),
        ],
        compiler_params=_params(("parallel", "arbitrary")),
        name="hg_core",
    )(q, k, v, gate, lf, onorm)


def _gelu(x):
    return 0.5 * x * (1.0 + lax.erf(x * (1.0 / math.sqrt(2.0))))


def _sgu_body(x_ref, g_ref, win_ref, vn_ref, ws_ref, bsT_ref, wout_ref, o_ref, a_s, *, d, tm):
    x = x_ref[...]
    h = _rms(x, g_ref[...]).astype(BF16)
    v = _gelu(_dot(h, win_ref[:, d:2 * d]))
    v = _rms(v, vn_ref[...]).astype(BF16)
    gd = d // SG_GROUPS
    ri = lax.broadcasted_iota(jnp.int32, (SG_CHUNK, SG_CHUNK), 0)
    ci = lax.broadcasted_iota(jnp.int32, (SG_CHUNK, SG_CHUNK), 1)
    bsT = bsT_ref[...]
    for pair in range(SG_GROUPS // 2):
        u2 = _gelu(_dot(h, win_ref[:, 2 * pair * gd:2 * (pair + 1) * gd]))
        for half in range(2):
            gidx = 2 * pair + half
            cols = slice(gidx * gd, (gidx + 1) * gd)
            w = jnp.where(ri >= ci, ws_ref[gidx], 0.0).astype(BF16)
            bias = bsT[:, gidx:gidx + 1]
            for c in range(tm // SG_CHUNK):
                rows = slice(c * SG_CHUNK, (c + 1) * SG_CHUNK)
                mixed = _dot(w, v[rows, cols]) + bias
                a_s[rows, cols] = (u2[rows, half * gd:(half + 1) * gd] * mixed).astype(BF16)
    o_ref[...] = x + _dot(a_s[...], wout_ref[...])


def _sgu(x, g, w_in, vnorm, w_s, b_sT, w_out, j):
    b, s, d = x.shape
    tm = TOK_TILE
    tok = pl.BlockSpec((None, tm, d), lambda i, j: (i, j, 0))
    return pl.pallas_call(
        functools.partial(_sgu_body, d=d, tm=tm),
        grid=(b, s // tm),
        in_specs=[tok, _resident((1, d)), _layer(w_in, j), _resident((1, d)),
                  _resident(w_s.shape), _resident(b_sT.shape), _layer(w_out, j)],
        out_specs=tok,
        out_shape=jax.ShapeDtypeStruct((b, s, d), F32),
        scratch_shapes=[pltpu.VMEM((tm, d), BF16)],
        compiler_params=_params(("parallel", "parallel")),
        name="sgu",
    )(x, g, w_in, vnorm, w_s, b_sT, w_out)


def _kv_mem_body(mem_ref, g_ref, w_ref, k_ref, v_ref, *, d):
    h = _rms(mem_ref[...], g_ref[...]).astype(BF16)
    k_ref[...] = _dot(h, w_ref[:, 0:d]).astype(BF16)
    v_ref[...] = _dot(h, w_ref[:, d:2 * d]).astype(BF16)


def _kv_mem(mem, g, w_kv):
    b, m, d = mem.shape
    depth = w_kv.shape[0]
    return pl.pallas_call(
        functools.partial(_kv_mem_body, d=d),
        grid=(depth, b),
        in_specs=[
            pl.BlockSpec((None, m, d), lambda l, i: (i, 0, 0)),
            _resident((1, d)),
            pl.BlockSpec((None, d, 2 * d), lambda l, i: (l, 0, 0)),
        ],
        out_specs=[
            pl.BlockSpec((None, None, m, d), lambda l, i: (l, i, 0, 0)),
            pl.BlockSpec((None, None, m, d), lambda l, i: (l, i, 0, 0)),
        ],
        out_shape=[
            jax.ShapeDtypeStruct((depth, b, m, d), BF16),
            jax.ShapeDtypeStruct((depth, b, m, d), BF16),
        ],
        compiler_params=_params(("parallel", "parallel")),
        name="kv_mem",
    )(mem, g, w_kv)


def _cross_body(*refs, d, q_scale, fused_proj):
    if fused_proj:
        x_ref, ap_ref, wp_ref, g_ref, wq_ref, k_ref, v_ref, wo_ref, o_ref, a_s, q_s, x_s = refs
    else:
        x_ref, g_ref, wq_ref, k_ref, v_ref, wo_ref, o_ref, a_s, q_s, x_s = refs
    hd = d // CA_HEADS
    tm = x_ref.shape[0]
    parts = [slice(i * CA_ROWS, (i + 1) * CA_ROWS) for i in range(tm // CA_ROWS)]
    for rows in parts:
        x = x_ref[rows, :]
        if fused_proj:
            x = x + _dot(ap_ref[rows, :], wp_ref[...])
        x_s[rows, :] = x
        h = _rms(x, g_ref[...]).astype(BF16)
        q_s[rows, :] = (_dot(h, wq_ref[...]) * q_scale).astype(BF16)
    for rows in parts:
        for hh in range(CA_HEADS):
            cols = slice(hh * hd, (hh + 1) * hd)
            s = _dot_nt(q_s[rows, cols], k_ref[:, cols])
            m = jnp.max(s, axis=-1, keepdims=True)
            p = jnp.exp2(s - m)
            l = jnp.sum(p, axis=-1, keepdims=True)
            o = _dot(p.astype(BF16), v_ref[:, cols]) * (1.0 / l)
            a_s[rows, cols] = o.astype(BF16)
    for rows in parts:
        o_ref[rows, :] = x_s[rows, :] + _dot(a_s[rows, :], wo_ref[...])


def _cross(x, g, wq, k, v, wo, layer, proj=None):
    b, s, d = x.shape
    m = v.shape[2]
    tm = 2 * TOK_TILE
    tok = pl.BlockSpec((None, tm, d), lambda i, j: (i, j, 0))
    q_scale = (d // CA_HEADS) ** -0.5 * LOG2E
    pre_specs, pre_args = ([tok, _layer(proj[1], proj[2])], list(proj[:2])) if proj is not None else ([], [])
    return pl.pallas_call(
        functools.partial(_cross_body, d=d, q_scale=q_scale, fused_proj=proj is not None),
        grid=(b, s // tm),
        in_specs=[tok] + pre_specs + [
            _resident((1, d)), _layer(wq, layer),
            pl.BlockSpec((None, None, m, d), lambda i, j: (layer, i, 0, 0)),
            pl.BlockSpec((None, None, m, d), lambda i, j: (layer, i, 0, 0)),
            _layer(wo, layer),
        ],
        out_specs=tok,
        out_shape=jax.ShapeDtypeStruct((b, s, d), F32),
        scratch_shapes=[pltpu.VMEM((tm, d), BF16), pltpu.VMEM((tm, d), BF16), pltpu.VMEM((tm, d), F32)],
        compiler_params=_params(("parallel", "parallel")),
        name="cross",
    )(x, *pre_args, g, wq, k, v, wo)


def _ffn_body(x_ref, g_ref, wgu_ref, wd_ref, gf_ref, o_ref, a_s, *, ff, final_norm):
    x = x_ref[...]
    h = _rms(x, g_ref[...]).astype(BF16)
    fc = FFN_CHUNK
    for c in range(ff // fc):
        gate = _dot(h, wgu_ref[:, c * fc:(c + 1) * fc])
        up = _dot(h, wgu_ref[:, ff + c * fc:ff + (c + 1) * fc])
        a_s[:, c * fc:(c + 1) * fc] = (_silu(gate) * up).astype(BF16)
    y = x + _dot(a_s[...], wd_ref[...])
    if final_norm:
        y = _rms(y, gf_ref[...])
    o_ref[...] = y


def _ffn(x, g, wgu, wd, layer, gf, final_norm):
    b, s, d = x.shape
    ff = wd.shape[1]
    tm = TOK_TILE
    tok = pl.BlockSpec((None, tm, d), lambda i, j: (i, j, 0))
    return pl.pallas_call(
        functools.partial(_ffn_body, ff=ff, final_norm=final_norm),
        grid=(b, s // tm),
        in_specs=[tok, _resident((1, d)), _layer(wgu, layer), _layer(wd, layer), _resident((1, d))],
        out_specs=tok,
        out_shape=jax.ShapeDtypeStruct((b, s, d), F32),
        scratch_shapes=[pltpu.VMEM((tm, ff), BF16)],
        compiler_params=_params(("parallel", "parallel")),
        name="ffn",
    )(x, g, wgu, wd, gf)


def kernel(x, mem, rel_bias, norm_mix, norm_cross, norm_ffn, norm_mem, norm_final,
           da_w_in, da_w_out, da_lq1, da_lk1, da_lq2, da_lk2, da_subln,
           hg_w_in, hg_w_out, hg_lower_bounds, hg_onorm,
           sg_w_in, sg_w_out, sg_vnorm, sg_w_s, sg_b_s,
           ca_w_q, ca_w_kv, ca_w_o, ffn_w_gu, ffn_w_down):
    depth = norm_mix.shape[0]
    row = lambda vec: vec.reshape(1, -1)
    bf = lambda w: w.astype(BF16)

    k_mem, v_mem = _kv_mem(mem, row(norm_mem), bf(ca_w_kv))
    bias = _da_bias(rel_bias, ATT_TILE)
    da_scale = DA_QK_DIM ** -0.5 * LOG2E
    da_w_in, da_w_out, hg_w_in, hg_w_out = bf(da_w_in), bf(da_w_out), bf(hg_w_in), bf(hg_w_out)
    sg_w_in, sg_w_out = bf(sg_w_in), bf(sg_w_out)
    ca_w_q, ca_w_o, ffn_w_gu, ffn_w_down = bf(ca_w_q), bf(ca_w_o), bf(ffn_w_gu), bf(ffn_w_down)

    for i in range(depth):
        kind, j = i % N_MIXERS, i // N_MIXERS
        g_mix = row(norm_mix[i])
        if kind == 0:
            lam_init = 0.8 - 0.6 * math.exp(-0.3 * i)
            qT, k, vT = _da_in(x, g_mix, da_w_in, j, da_scale)
            lv = jnp.stack([da_lq1[j], da_lk1[j], da_lq2[j], da_lk2[j]])
            a = _da_attn(lv, row(da_subln[j]), qT, k, vT, bias, lam_init)
            proj = (a, da_w_out, j)
        elif kind == 1:
            q, k, v, gate, lf = _hg_in(x, g_mix, hg_w_in, j, hg_lower_bounds, i)
            a = _hg_core(q, k, v, gate, lf, row(hg_onorm[j]))
            proj = (a, hg_w_out, j)
        else:
            x = _sgu(x, g_mix, sg_w_in, row(sg_vnorm[j]), sg_w_s[j], sg_b_s[j].T, sg_w_out, j)
            proj = None
        x = _cross(x, row(norm_cross[i]), ca_w_q, k_mem, v_mem, ca_w_o, i, proj)
        x = _ffn(x, row(norm_ffn[i]), ffn_w_gu, ffn_w_down, i, row(norm_final), i == depth - 1)
    return x
```

```python
import functools
import math

import numpy as np
import jax
import jax.numpy as jnp
from jax import lax
from jax.experimental import pallas as pl
from jax.experimental.pallas import tpu as pltpu

F32 = jnp.float32
BF16 = jnp.bfloat16

EPS = 1e-6
LOG2E = math.log2(math.e)
NEG_BIG = -1e30

DA_QK_DIM = 64
DA_V_DIM = 128
DA_SUM_ROWS = 16
REL_BUCKETS = 32
REL_MAX_DIST = 128
HG_DIM = 128
HG_CHUNK = 64
HG_SUB = 16
HG_MAX_EXPONENT = 80.0
SG_CHUNK = 128
SG_GROUPS = 8
CA_HEADS = 4
N_MIXERS = 3

VMEM_LIMIT = 56 * 1024 * 1024
TOK_TILE = 1024
ATT_TILE = 512
ATT_Q_BLOCKS = 4
ATT_STRIP = 256
HG_TILE = 512
FFN_CHUNK = 256
CA_ROWS = 256
HG_IN_ROWS = 256


def _params(sem):
    return pltpu.CompilerParams(dimension_semantics=sem, vmem_limit_bytes=VMEM_LIMIT)


def _resident(shape):
    nd = len(shape)
    return pl.BlockSpec(shape, lambda *_: (0,) * nd, pipeline_mode=pl.Buffered(1))


def _layer(w, layer):
    nd = w.ndim - 1
    return pl.BlockSpec((None,) + w.shape[1:], lambda *_: (layer,) + (0,) * nd, pipeline_mode=pl.Buffered(1))


def _rms(x, g):
    ms = jnp.mean(x * x, axis=-1, keepdims=True)
    return x * lax.rsqrt(ms + EPS) * g


def _dot(a, b):
    return jnp.dot(a, b, preferred_element_type=F32)


def _dot_nt(a, b):
    return lax.dot_general(a, b, (((1,), (1,)), ((), ())), preferred_element_type=F32)


def _dot_tn(a, b):
    return lax.dot_general(a, b, (((0,), (0,)), ((), ())), preferred_element_type=F32)


def _silu(x):
    return x * (1.0 / (1.0 + jnp.exp(-x)))


def _da_in_body(x_ref, g_ref, w_ref, qT_ref, k_ref, vT_ref, t_s, *, d, q_scale):
    h = _rms(x_ref[...], g_ref[...]).astype(BF16)
    t_s[...] = _dot(h, w_ref[:, 0:d]) * q_scale
    qT_ref[...] = t_s[...].T.astype(BF16)
    k_ref[...] = _dot(h, w_ref[:, d:2 * d]).astype(BF16)
    t_s[...] = _dot(h, w_ref[:, 2 * d:3 * d])
    vT_ref[...] = t_s[...].T.astype(BF16)


def _da_in(x, g, w, j, q_scale):
    b, s, d = x.shape
    tm = TOK_TILE
    return pl.pallas_call(
        functools.partial(_da_in_body, d=d, q_scale=q_scale),
        grid=(b, s // tm),
        in_specs=[
            pl.BlockSpec((None, tm, d), lambda i, j: (i, j, 0)),
            _resident((1, d)),
            _layer(w, j),
        ],
        out_specs=[
            pl.BlockSpec((None, d, tm), lambda i, j: (i, 0, j)),
            pl.BlockSpec((None, tm, d), lambda i, j: (i, j, 0)),
            pl.BlockSpec((None, d, tm), lambda i, j: (i, 0, j)),
        ],
        out_shape=[
            jax.ShapeDtypeStruct((b, d, s), BF16),
            jax.ShapeDtypeStruct((b, s, d), BF16),
            jax.ShapeDtypeStruct((b, d, s), BF16),
        ],
        scratch_shapes=[pltpu.VMEM((tm, d), F32)],
        compiler_params=_params(("parallel", "parallel")),
        name="da_in",
    )(x, g, w)


def _rel_bucket_np(dist):
    n = np.maximum(dist, 0)
    exact = REL_BUCKETS // 2
    nf = np.maximum(n, exact).astype(np.float32)
    large = exact + (np.log(nf / np.float32(exact)) / np.float32(math.log(REL_MAX_DIST / exact))
                     * np.float32(REL_BUCKETS - exact)).astype(np.int32)
    large = np.minimum(large, REL_BUCKETS - 1)
    return np.where(n < exact, n, large).astype(np.int32)


def _bucket_tiles(t):
    kk = np.arange(t, dtype=np.int64)[:, None]
    qq = np.arange(t, dtype=np.int64)[None, :]
    d_diag = qq - kk
    diag = np.where(d_diag >= 0, _rel_bucket_np(d_diag), -1).astype(np.int32)
    off = _rel_bucket_np(qq + t - kk)
    assert int(_rel_bucket_np(np.arange(t + 1, 8 * t)).min()) == REL_BUCKETS - 1
    return diag, off


def _da_bias_body(rb_ref, bd_ref, bo_ref, out_ref, *, t):
    h = pl.program_id(0)
    far = rb_ref[h, REL_BUCKETS - 1]
    for which, bref in ((0, bd_ref), (1, bo_ref)):
        bk = bref[...]
        acc = jnp.zeros((t, t), F32)
        for bb in range(REL_BUCKETS - 1):
            acc = jnp.where(bk == bb, rb_ref[h, bb] - far, acc)
        out_ref[which] = jnp.where(bk < 0, NEG_BIG, acc * LOG2E)


def _da_bias(rel_bias, t):
    diag, off = _bucket_tiles(t)
    hh = rel_bias.shape[0]
    return pl.pallas_call(
        functools.partial(_da_bias_body, t=t),
        grid=(hh,),
        in_specs=[
            pl.BlockSpec(memory_space=pltpu.SMEM),
            _resident((t, t)),
            _resident((t, t)),
        ],
        out_specs=pl.BlockSpec((None, 2, t, t), lambda i: (i, 0, 0, 0)),
        out_shape=jax.ShapeDtypeStruct((hh, 2, t, t), F32),
        compiler_params=_params(("parallel",)),
        name="da_bias",
    )(rel_bias, jnp.asarray(diag), jnp.asarray(off))


def _da_attn_body(lv_ref, sub_ref, qT_ref, k_ref, vT_ref, bias_ref, o_ref,
                  qb_s, m_s, acc_s, s0_s, s1_s, mx0_s, mx1_s, *, tq, tk, nq, lam_init):
    sw = ATT_STRIP
    per_half = tk // sw

    def load_q(qi):
        qT = qT_ref[:, pl.ds(pl.multiple_of(qi * tq, tq), tq)]
        row = lax.broadcasted_iota(jnp.int32, qT.shape, 0)
        zero = jnp.zeros_like(qT)
        qb_s[:, 0:tq] = jnp.where(row < DA_QK_DIM, qT, zero)
        qb_s[:, tq:2 * tq] = jnp.where(row >= DA_QK_DIM, qT, zero)

    def reset():
        m_s[...] = jnp.full(m_s.shape, NEG_BIG, F32)
        acc_s[...] = jnp.zeros(acc_s.shape, F32)

    lv = lv_ref[...]
    s1 = jnp.sum(lv[0:1] * lv[1:2], axis=-1, keepdims=True)
    s2 = jnp.sum(lv[2:3] * lv[3:4], axis=-1, keepdims=True)
    lam = jnp.exp(s1) - jnp.exp(s2) + lam_init

    def finalize(qi):
        o2 = acc_s[0:DA_V_DIM, :] * (1.0 / acc_s[DA_V_DIM:DA_V_DIM + 1, :])
        oT = o2[:, 0:tq] - lam * o2[:, tq:2 * tq]
        o = _rms(oT.T, sub_ref[...]) * (1.0 - lam_init)
        o_ref[pl.ds(pl.multiple_of(qi * tq, tq), tq), :] = o.astype(BF16)

    nblk = tq // tk
    kinds = {"far": (None,) * nblk, "pre": (1,) + (None,) * (nblk - 1)}
    for t in range(nblk):
        kinds[t] = tuple("skip" if r < t else 0 if r == t else 1 if r == t + 1 else None for r in range(nblk))
    bufs_of = lambda kj: b0 if kj % 2 == 0 else b1

    def strips(kind):
        for comp in range(2):
            for r in range(tq // sw):
                which = kinds[kind][r // per_half]
                if which == "skip":
                    continue
                qoff = (r % per_half) * sw
                nk = min(qoff + sw, tk) if which == 0 else tk
                yield slice(comp * tq + r * sw, comp * tq + (r + 1) * sw), which, qoff, nk

    def a_strip(kt, bufs, cols, which, qoff, nk):
        buf, mx = bufs
        s = _dot(kt[0:nk, :], qb_s[:, cols])
        if which is not None:
            s = s + bias_ref[which, 0:nk, qoff:qoff + sw]
        buf[0:nk, cols] = s
        mx[:, cols] = jnp.max(s, axis=0, keepdims=True)

    def b_strip(vt, bufs, cols, nk):
        buf, mx = bufs
        m_old = m_s[:, cols]
        m_new = jnp.maximum(m_old, mx[:, cols])
        alpha = jnp.exp2(m_old - m_new)
        p = jnp.exp2(buf[0:nk, cols] - m_new)
        acc_s[:, cols] = alpha * acc_s[:, cols] + _dot(vt[:, 0:nk], p.astype(BF16))
        m_s[:, cols] = m_new

    def step(a=None, b=None):
        sa, sb = [], []
        if a is not None:
            kj, kind, bufs_a = a
            kt = k_ref[pl.ds(pl.multiple_of(kj * tk, tk), tk), :]
            sa = list(strips(kind))
        if b is not None:
            kj, kind, bufs_b = b
            vt = vT_ref[:, pl.ds(pl.multiple_of(kj * tk, tk), tk)]
            vt = jnp.concatenate([vt, jnp.ones((DA_SUM_ROWS, tk), BF16)], axis=0)
            sb = list(strips(kind))
        for n in range(max(len(sa), len(sb))):
            if n < len(sa):
                a_strip(kt, bufs_a, *sa[n])
            if n < len(sb):
                b_strip(vt, bufs_b, sb[n][0], sb[n][3])

    b0, b1 = (s0_s, mx0_s), (s1_s, mx1_s)

    load_q(0)
    reset()
    step(a=(0, 0, b0))
    for t in range(1, nblk):
        step(a=(t, t, bufs_of(t)), b=(t - 1, t - 1, bufs_of(t - 1)))

    def far_steps(base, count):
        for t in range(count):
            even = t % 2 == 0
            step(a=(base + t + 1, "far", b1 if even else b0), b=(base + t, "far", b0 if even else b1))

    def far4(i, carry):
        far_steps(4 * i, 4)
        return carry

    def q_tile(qi, carry):
        k0 = nblk * qi
        load_q(qi)
        step(a=(0, "far", b0), b=(k0 - 1, nblk - 1, b1))
        finalize(qi - 1)
        reset()
        n_pairs = (nblk // 2) * qi - 1
        lax.fori_loop(0, n_pairs // 2, far4, 0)

        @pl.when(n_pairs % 2 == 1)
        def _():
            far_steps(2 * (n_pairs - 1), 2)

        step(a=(k0 - 1, "pre", b1), b=(k0 - 2, "far", b0))
        step(a=(k0, 0, b0), b=(k0 - 1, "pre", b1))
        for t in range(1, nblk):
            step(a=(k0 + t, t, bufs_of(t)), b=(k0 + t - 1, t - 1, bufs_of(t - 1)))
        return carry

    lax.fori_loop(1, nq, q_tile, 0)
    step(b=(nblk * nq - 1, nblk - 1, b1))
    finalize(nq - 1)


def _da_attn(lv, subln, qT, k, vT, bias, lam_init):
    b, d, s = qT.shape
    tq, tk = ATT_Q_BLOCKS * ATT_TILE, ATT_TILE
    hd = DA_V_DIM
    nh = d // hd
    return pl.pallas_call(
        functools.partial(_da_attn_body, tq=tq, tk=tk, nq=s // tq, lam_init=lam_init),
        grid=(b, nh),
        in_specs=[
            _resident(lv.shape),
            _resident(subln.shape),
            pl.BlockSpec((None, hd, s), lambda i, h: (i, h, 0)),
            pl.BlockSpec((None, s, hd), lambda i, h: (i, 0, h)),
            pl.BlockSpec((None, hd, s), lambda i, h: (i, h, 0)),
            pl.BlockSpec((None, 2, tk, tk), lambda i, h: (h, 0, 0, 0)),
        ],
        out_specs=pl.BlockSpec((None, s, hd), lambda i, h: (i, 0, h)),
        out_shape=jax.ShapeDtypeStruct((b, s, d), BF16),
        scratch_shapes=[
            pltpu.VMEM((hd, 2 * tq), BF16),
            pltpu.VMEM((1, 2 * tq), F32),
            pltpu.VMEM((hd + DA_SUM_ROWS, 2 * tq), F32),
            pltpu.VMEM((tk, 2 * tq), F32),
            pltpu.VMEM((tk, 2 * tq), F32),
            pltpu.VMEM((1, 2 * tq), F32),
            pltpu.VMEM((1, 2 * tq), F32),
        ],
        compiler_params=_params(("parallel", "parallel")),
        name="da_attn",
    )(lv, subln, qT, k, vT, bias)


def _hg_in_body(x_ref, g_ref, w_ref, lbs_ref, q_ref, k_ref, v_ref, gate_ref, lf_ref, *, d, layer):
    lbs = lbs_ref[...]
    e = jnp.exp(lbs - jnp.max(lbs, axis=0, keepdims=True))
    sm = e / jnp.sum(e, axis=0, keepdims=True)
    lb = jnp.sum(sm[0:layer + 1], axis=0, keepdims=True) - sm[0:1]
    a = jnp.log(lb)
    log_1m_lb = jnp.log(1.0 - lb)
    tm = x_ref.shape[0]
    for i in range(tm // HG_IN_ROWS):
        rows = slice(i * HG_IN_ROWS, (i + 1) * HG_IN_ROWS)
        h = _rms(x_ref[rows, :], g_ref[...]).astype(BF16)
        q = _dot(h, w_ref[:, 0:d])
        q_ref[rows, :] = _silu(q).astype(BF16)
        f = _dot(h, w_ref[:, d:2 * d])
        log_sig = -(jnp.maximum(-f, 0.0) + jnp.log(1.0 + jnp.exp(-jnp.abs(f))))
        c = log_1m_lb + log_sig
        log_f = jnp.maximum(a, c) + jnp.log(1.0 + jnp.exp(-jnp.abs(a - c)))
        lf_ref[rows, :] = log_f
        k_ref[rows, :] = (1.0 - jnp.exp(log_f)).astype(BF16)
        v_ref[rows, :] = _dot(h, w_ref[:, 2 * d:3 * d]).astype(BF16)
        gate_ref[rows, :] = _dot(h, w_ref[:, 3 * d:4 * d]).astype(BF16)


def _hg_in(x, g, w, j, lbs, layer):
    b, s, d = x.shape
    tm = TOK_TILE
    tok = pl.BlockSpec((None, tm, d), lambda i, j: (i, j, 0))
    return pl.pallas_call(
        functools.partial(_hg_in_body, d=d, layer=layer),
        grid=(b, s // tm),
        in_specs=[tok, _resident((1, d)), _layer(w, j), _resident(lbs.shape)],
        out_specs=[tok, tok, tok, tok, tok],
        out_shape=[jax.ShapeDtypeStruct((b, s, d), BF16)] * 4 + [jax.ShapeDtypeStruct((b, s, d), F32)],
        compiler_params=_params(("parallel", "parallel")),
        name="hg_in",
    )(x, g, w, lbs)


def _hg_core_body(q_ref, k_ref, v_ref, gate_ref, lf_ref, on_ref, o_ref,
                  st_s, g_s, q_s, k_s, v_s, o_s, a_s, *, tile, nh):
    c64, c16, hd = HG_CHUNK, HG_SUB, HG_DIM
    c8 = c16 // 2
    nsub = c64 // c16
    heads = [slice(h * hd, (h + 1) * hd) for h in range(nh)]

    @pl.when(pl.program_id(1) == 0)
    def _():
        st_s[...] = jnp.zeros(st_s.shape, F32)

    ri = lax.broadcasted_iota(jnp.int32, (c64, c64), 0)
    ci = lax.broadcasted_iota(jnp.int32, (c64, c64), 1)
    tri = jnp.where(ri >= ci, 1.0, 0.0).astype(BF16)
    tt = lax.broadcasted_iota(jnp.int32, (c8, 1), 0)
    on = on_ref[...]

    def chunk(c, carry, fast):
        base = pl.multiple_of(c * c64, c64)
        rows = pl.ds(base, c64)
        lf = lf_ref[rows, :]
        hi = lf.astype(BF16)
        r1 = lf - hi.astype(F32)
        mid = r1.astype(BF16)
        lo = (r1 - mid.astype(F32)).astype(BF16)
        gc = _dot(tri, hi) + _dot(tri, mid) + _dot(tri, lo)
        q = q_ref[rows, :].astype(F32)
        k = k_ref[rows, :].astype(F32)
        vb = v_ref[rows, :]
        g_s[...] = gc
        q_s[...] = q
        k_s[...] = k
        if not fast:
            v_s[...] = vb.astype(F32)
        g_last = gc[c64 - 1:c64, :]

        qg = (q * jnp.exp(gc)).astype(BF16)
        kd = (k * jnp.exp(g_last - gc)).astype(BF16)
        dec = jnp.exp(g_last)

        a_s[...] = jnp.zeros(a_s.shape, BF16)
        for i in range(0 if fast else 1, nsub):
            lo_r, hi_r = i * c16, (i + 1) * c16
            nk = hi_r if fast else lo_r
            if i == 0:
                qi = q_s[lo_r:hi_r, :] * jnp.exp(g_s[lo_r:hi_r, :])
                kj = k_s[0:nk, :] * jnp.exp(-g_s[0:nk, :])
            else:
                ref = g_s[lo_r - 1:lo_r, :]
                qi = q_s[lo_r:hi_r, :] * jnp.exp(g_s[lo_r:hi_r, :] - ref)
                kj = k_s[0:nk, :] * jnp.exp(ref - g_s[0:nk, :])
            qi = qi.astype(BF16)
            kj = kj.astype(BF16)
            if fast:
                keep = (lax.broadcasted_iota(jnp.int32, (c16, nk), 1)
                        <= lax.broadcasted_iota(jnp.int32, (c16, nk), 0) + lo_r)
            for h, cols in enumerate(heads):
                a = _dot_nt(qi[:, cols], kj[:, cols])
                if fast:
                    a = jnp.where(keep, a, 0.0)
                a_s[h, lo_r:hi_r, 0:nk] = a.astype(BF16)

        for h, cols in enumerate(heads):
            st = st_s[h]
            o_s[:, cols] = _dot_nt(qg[:, cols], st.astype(BF16)) + _dot(a_s[h], vb[:, cols])
            st_s[h] = st * dec[:, cols] + _dot_tn(vb[:, cols], kd[:, cols])

        grp = max(nh // 2, 1)
        if not fast:
            for i in range(nsub):
                for g0 in range(0, nh, grp):
                    wide = slice(g0 * hd, (g0 + grp) * hd)
                    sub = [slice(h * hd, (h + 1) * hd) for h in range(grp)]
                    r0, r1_, r2 = i * c16, i * c16 + c8, (i + 1) * c16
                    g_lo, g_hi = g_s[r0:r1_, wide], g_s[r1_:r2, wide]
                    q_lo, q_hi = q_s[r0:r1_, wide], q_s[r1_:r2, wide]
                    acc_lo = [None] * grp
                    acc_hi = [None] * grp
                    for s in range(c16):
                        j = r0 + s
                        gs, ks, vs = g_s[j:j + 1, wide], k_s[j:j + 1, wide], v_s[j:j + 1, wide]
                        pr_hi = q_hi * ks * jnp.exp(g_hi - gs)
                        if s < c8:
                            pr_lo = q_lo * ks * jnp.exp(g_lo - gs)
                        for h, cols in enumerate(sub):
                            a = jnp.sum(pr_hi[:, cols], axis=-1, keepdims=True)
                            if s >= c8:
                                a = jnp.where(tt >= s - c8, a, 0.0)
                            t_hi = a * vs[:, cols]
                            acc_hi[h] = t_hi if acc_hi[h] is None else acc_hi[h] + t_hi
                            if s < c8:
                                a = jnp.sum(pr_lo[:, cols], axis=-1, keepdims=True)
                                a = jnp.where(tt >= s, a, 0.0)
                                t_lo = a * vs[:, cols]
                                acc_lo[h] = t_lo if acc_lo[h] is None else acc_lo[h] + t_lo
                    for h in range(grp):
                        cols = slice((g0 + h) * hd, (g0 + h + 1) * hd)
                        o_s[r0:r1_, cols] += acc_lo[h]
                        o_s[r1_:r2, cols] += acc_hi[h]

        gate = gate_ref[rows, :].astype(F32)
        sg = _silu(gate)
        for cols in heads:
            o = _rms(o_s[:, cols], on)
            o_ref[rows, cols] = (o * sg[:, cols]).astype(BF16)
        return carry

    fast = jnp.min(lf_ref[...]) >= -HG_MAX_EXPONENT / c16

    @pl.when(fast)
    def _():
        lax.fori_loop(0, tile // c64, functools.partial(chunk, fast=True), 0, unroll=True)

    @pl.when(jnp.logical_not(fast))
    def _():
        lax.fori_loop(0, tile // c64, functools.partial(chunk, fast=False), 0)


def _hg_core(q, k, v, gate, lf, onorm):
    b, s, d = q.shape
    hd = HG_DIM
    nh = d // hd
    tile = HG_TILE
    blk = pl.BlockSpec((None, tile, d), lambda i, j: (i, j, 0))
    return pl.pallas_call(
        functools.partial(_hg_core_body, tile=tile, nh=nh),
        grid=(b, s // tile),
        in_specs=[blk, blk, blk, blk, blk, _resident(onorm.shape)],
        out_specs=blk,
        out_shape=jax.ShapeDtypeStruct((b, s, d), BF16),
        scratch_shapes=[
            pltpu.VMEM((nh, hd, hd), F32),
            pltpu.VMEM((HG_CHUNK, d), F32),
            pltpu.VMEM((HG_CHUNK, d), F32),
            pltpu.VMEM((HG_CHUNK, d), F32),
            pltpu.VMEM((HG_CHUNK, d), F32),
            pltpu.VMEM((HG_CHUNK, d), F32),
            pltpu.VMEM((nh, HG_CHUNK, HG_CHUNK), BF16),
        ],
        compiler_params=_params(("parallel", "arbitrary")),
        name="hg_core",
    )(q, k, v, gate, lf, onorm)


def _gelu(x):
    return 0.5 * x * (1.0 + lax.erf(x * (1.0 / math.sqrt(2.0))))


def _sgu_body(x_ref, g_ref, win_ref, vn_ref, ws_ref, bsT_ref, wout_ref, o_ref, a_s, *, d, tm):
    x = x_ref[...]
    h = _rms(x, g_ref[...]).astype(BF16)
    v = _gelu(_dot(h, win_ref[:, d:2 * d]))
    v = _rms(v, vn_ref[...]).astype(BF16)
    gd = d // SG_GROUPS
    ri = lax.broadcasted_iota(jnp.int32, (SG_CHUNK, SG_CHUNK), 0)
    ci = lax.broadcasted_iota(jnp.int32, (SG_CHUNK, SG_CHUNK), 1)
    bsT = bsT_ref[...]
    for pair in range(SG_GROUPS // 2):
        u2 = _gelu(_dot(h, win_ref[:, 2 * pair * gd:2 * (pair + 1) * gd]))
        for half in range(2):
            gidx = 2 * pair + half
            cols = slice(gidx * gd, (gidx + 1) * gd)
            w = jnp.where(ri >= ci, ws_ref[gidx], 0.0).astype(BF16)
            bias = bsT[:, gidx:gidx + 1]
            for c in range(tm // SG_CHUNK):
                rows = slice(c * SG_CHUNK, (c + 1) * SG_CHUNK)
                mixed = _dot(w, v[rows, cols]) + bias
                a_s[rows, cols] = (u2[rows, half * gd:(half + 1) * gd] * mixed).astype(BF16)
    o_ref[...] = x + _dot(a_s[...], wout_ref[...])


def _sgu(x, g, w_in, vnorm, w_s, b_sT, w_out, j):
    b, s, d = x.shape
    tm = TOK_TILE
    tok = pl.BlockSpec((None, tm, d), lambda i, j: (i, j, 0))
    return pl.pallas_call(
        functools.partial(_sgu_body, d=d, tm=tm),
        grid=(b, s // tm),
        in_specs=[tok, _resident((1, d)), _layer(w_in, j), _resident((1, d)),
                  _resident(w_s.shape), _resident(b_sT.shape), _layer(w_out, j)],
        out_specs=tok,
        out_shape=jax.ShapeDtypeStruct((b, s, d), F32),
        scratch_shapes=[pltpu.VMEM((tm, d), BF16)],
        compiler_params=_params(("parallel", "parallel")),
        name="sgu",
    )(x, g, w_in, vnorm, w_s, b_sT, w_out)


def _kv_mem_body(mem_ref, g_ref, w_ref, k_ref, v_ref, *, d):
    h = _rms(mem_ref[...], g_ref[...]).astype(BF16)
    k_ref[...] = _dot(h, w_ref[:, 0:d]).astype(BF16)
    v_ref[...] = _dot(h, w_ref[:, d:2 * d]).astype(BF16)


def _kv_mem(mem, g, w_kv):
    b, m, d = mem.shape
    depth = w_kv.shape[0]
    return pl.pallas_call(
        functools.partial(_kv_mem_body, d=d),
        grid=(depth, b),
        in_specs=[
            pl.BlockSpec((None, m, d), lambda l, i: (i, 0, 0)),
            _resident((1, d)),
            pl.BlockSpec((None, d, 2 * d), lambda l, i: (l, 0, 0)),
        ],
        out_specs=[
            pl.BlockSpec((None, None, m, d), lambda l, i: (l, i, 0, 0)),
            pl.BlockSpec((None, None, m, d), lambda l, i: (l, i, 0, 0)),
        ],
        out_shape=[
            jax.ShapeDtypeStruct((depth, b, m, d), BF16),
            jax.ShapeDtypeStruct((depth, b, m, d), BF16),
        ],
        compiler_params=_params(("parallel", "parallel")),
        name="kv_mem",
    )(mem, g, w_kv)


def _cross_body(*refs, d, q_scale, fused_proj):
    if fused_proj:
        x_ref, ap_ref, wp_ref, g_ref, wq_ref, k_ref, v_ref, wo_ref, o_ref, a_s, q_s, x_s = refs
    else:
        x_ref, g_ref, wq_ref, k_ref, v_ref, wo_ref, o_ref, a_s, q_s, x_s = refs
    hd = d // CA_HEADS
    tm = x_ref.shape[0]
    parts = [slice(i * CA_ROWS, (i + 1) * CA_ROWS) for i in range(tm // CA_ROWS)]
    for rows in parts:
        x = x_ref[rows, :]
        if fused_proj:
            x = x + _dot(ap_ref[rows, :], wp_ref[...])
        x_s[rows, :] = x
        h = _rms(x, g_ref[...]).astype(BF16)
        q_s[rows, :] = (_dot(h, wq_ref[...]) * q_scale).astype(BF16)
    for rows in parts:
        for hh in range(CA_HEADS):
            cols = slice(hh * hd, (hh + 1) * hd)
            s = _dot_nt(q_s[rows, cols], k_ref[:, cols])
            m = jnp.max(s, axis=-1, keepdims=True)
            p = jnp.exp2(s - m)
            l = jnp.sum(p, axis=-1, keepdims=True)
            o = _dot(p.astype(BF16), v_ref[:, cols]) * (1.0 / l)
            a_s[rows, cols] = o.astype(BF16)
    for rows in parts:
        o_ref[rows, :] = x_s[rows, :] + _dot(a_s[rows, :], wo_ref[...])


def _cross(x, g, wq, k, v, wo, layer, proj=None):
    b, s, d = x.shape
    m = v.shape[2]
    tm = TOK_TILE
    tok = pl.BlockSpec((None, tm, d), lambda i, j: (i, j, 0))
    q_scale = (d // CA_HEADS) ** -0.5 * LOG2E
    pre_specs, pre_args = ([tok, _layer(proj[1], proj[2])], list(proj[:2])) if proj is not None else ([], [])
    return pl.pallas_call(
        functools.partial(_cross_body, d=d, q_scale=q_scale, fused_proj=proj is not None),
        grid=(b, s // tm),
        in_specs=[tok] + pre_specs + [
            _resident((1, d)), _layer(wq, layer),
            pl.BlockSpec((None, None, m, d), lambda i, j: (layer, i, 0, 0)),
            pl.BlockSpec((None, None, m, d), lambda i, j: (layer, i, 0, 0)),
            _layer(wo, layer),
        ],
        out_specs=tok,
        out_shape=jax.ShapeDtypeStruct((b, s, d), F32),
        scratch_shapes=[pltpu.VMEM((tm, d), BF16), pltpu.VMEM((tm, d), BF16), pltpu.VMEM((tm, d), F32)],
        compiler_params=_params(("parallel", "parallel")),
        name="cross",
    )(x, *pre_args, g, wq, k, v, wo)


def _ffn_body(x_ref, g_ref, wgu_ref, wd_ref, gf_ref, o_ref, a_s, *, ff, final_norm):
    x = x_ref[...]
    h = _rms(x, g_ref[...]).astype(BF16)
    fc = FFN_CHUNK
    for c in range(ff // fc):
        gate = _dot(h, wgu_ref[:, c * fc:(c + 1) * fc])
        up = _dot(h, wgu_ref[:, ff + c * fc:ff + (c + 1) * fc])
        a_s[:, c * fc:(c + 1) * fc] = (_silu(gate) * up).astype(BF16)
    y = x + _dot(a_s[...], wd_ref[...])
    if final_norm:
        y = _rms(y, gf_ref[...])
    o_ref[...] = y


def _ffn(x, g, wgu, wd, layer, gf, final_norm):
    b, s, d = x.shape
    ff = wd.shape[1]
    tm = TOK_TILE
    tok = pl.BlockSpec((None, tm, d), lambda i, j: (i, j, 0))
    return pl.pallas_call(
        functools.partial(_ffn_body, ff=ff, final_norm=final_norm),
        grid=(b, s // tm),
        in_specs=[tok, _resident((1, d)), _layer(wgu, layer), _layer(wd, layer), _resident((1, d))],
        out_specs=tok,
        out_shape=jax.ShapeDtypeStruct((b, s, d), F32),
        scratch_shapes=[pltpu.VMEM((tm, ff), BF16)],
        compiler_params=_params(("parallel", "parallel")),
        name="ffn",
    )(x, g, wgu, wd, gf)


def kernel(x, mem, rel_bias, norm_mix, norm_cross, norm_ffn, norm_mem, norm_final,
           da_w_in, da_w_out, da_lq1, da_lk1, da_lq2, da_lk2, da_subln,
           hg_w_in, hg_w_out, hg_lower_bounds, hg_onorm,
           sg_w_in, sg_w_out, sg_vnorm, sg_w_s, sg_b_s,
           ca_w_q, ca_w_kv, ca_w_o, ffn_w_gu, ffn_w_down):
    depth = norm_mix.shape[0]
    row = lambda vec: vec.reshape(1, -1)
    bf = lambda w: w.astype(BF16)

    k_mem, v_mem = _kv_mem(mem, row(norm_mem), bf(ca_w_kv))
    bias = _da_bias(rel_bias, ATT_TILE)
    da_scale = DA_QK_DIM ** -0.5 * LOG2E
    da_w_in, da_w_out, hg_w_in, hg_w_out = bf(da_w_in), bf(da_w_out), bf(hg_w_in), bf(hg_w_out)
    sg_w_in, sg_w_out = bf(sg_w_in), bf(sg_w_out)
    ca_w_q, ca_w_o, ffn_w_gu, ffn_w_down = bf(ca_w_q), bf(ca_w_o), bf(ffn_w_gu), bf(ffn_w_down)

    for i in range(depth):
        kind, j = i % N_MIXERS, i // N_MIXERS
        g_mix = row(norm_mix[i])
        if kind == 0:
            lam_init = 0.8 - 0.6 * math.exp(-0.3 * i)
            qT, k, vT = _da_in(x, g_mix, da_w_in, j, da_scale)
            lv = jnp.stack([da_lq1[j], da_lk1[j], da_lq2[j], da_lk2[j]])
            a = _da_attn(lv, row(da_subln[j]), qT, k, vT, bias, lam_init)
            proj = (a, da_w_out, j)
        elif kind == 1:
            q, k, v, gate, lf = _hg_in(x, g_mix, hg_w_in, j, hg_lower_bounds, i)
            a = _hg_core(q, k, v, gate, lf, row(hg_onorm[j]))
            proj = (a, hg_w_out, j)
        else:
            x = _sgu(x, g_mix, sg_w_in, row(sg_vnorm[j]), sg_w_s[j], sg_b_s[j].T, sg_w_out, j)
            proj = None
        x = _cross(x, row(norm_cross[i]), ca_w_q, k_mem, v_mem, ca_w_o, i, proj)
        x = _ffn(x, row(norm_ffn[i]), ffn_w_gu, ffn_w_down, i, row(norm_final), i == depth - 1)
    return x
```
